```python
import math
import jax, jax.numpy as jnp
from jax import lax
import numpy as np

D_MODEL = 1024
BATCH = 8
SEQ = 2048
DEPTH = 2
DEC_BATCH = 128
DEC_SEQ = 8
PAST_LEN = 16384
PAGE_SIZE = 128

N_PAIRS = DEPTH // 2
BRANCH = D_MODEL
MIX_WIDTH = 2 * BRANCH
A_CONV = 3
B_CONV = 31
N_HEADS_C = 4
N_HEADS_D = 4
HEAD_DIM = BRANCH // N_HEADS_C
CHUNK = 128
ROPE_BASE = 10000.0
EPS = 1e-6
CONV_IN_COLS = 7 * BRANCH
REC_IN_COLS = 9 * BRANCH + 2 * N_HEADS_C

kernel_name = 'hybrid_conv_mlstm_retention_step'


def rmsnorm(x, g):
    xf = x.astype(jnp.float32)
    y = xf * lax.rsqrt(jnp.mean(xf * xf, axis=-1, keepdims=True) + EPS)
    return (y * g.astype(jnp.float32)).astype(x.dtype)


def layernorm(x, g, b):
    xf = x.astype(jnp.float32)
    mu = jnp.mean(xf, axis=-1, keepdims=True)
    var = jnp.mean(jnp.square(xf - mu), axis=-1, keepdims=True)
    y = (xf - mu) * lax.rsqrt(var + EPS)
    return (y * g.astype(jnp.float32) + b.astype(jnp.float32)).astype(x.dtype)


def headnorm(x, g, n_heads):
    bsz, t, _ = x.shape
    xs = x.reshape(bsz, t, n_heads, -1)
    mu = jnp.mean(xs, axis=-1, keepdims=True)
    var = jnp.mean(jnp.square(xs - mu), axis=-1, keepdims=True)
    y = ((xs - mu) * lax.rsqrt(var + EPS)).reshape(bsz, t, -1)
    return y * g.astype(jnp.float32)


def causal_dwconv(u, buf, w):
    xp = jnp.concatenate([buf.astype(u.dtype), u], axis=1)
    width, ch = w.shape
    out = lax.conv_general_dilated(xp, w.astype(u.dtype)[:, None, :], window_strides=(1,), padding='VALID',
                                   dimension_numbers=('NWC', 'WIO', 'NWC'), feature_group_count=ch)
    return out, xp[:, xp.shape[1] - (width - 1):]


def split_heads(x, n_heads):
    bsz, t, _ = x.shape
    return x.reshape(bsz, t, n_heads, -1).transpose(0, 2, 1, 3)


def merge_heads(x):
    bsz, h, t, d = x.shape
    return x.transpose(0, 2, 1, 3).reshape(bsz, t, h * d)


def to_chunks(a, n_chunks, length):
    a = a.reshape(a.shape[:2] + (n_chunks, length) + a.shape[3:])
    return jnp.moveaxis(a, 2, 0)


def from_chunks(a):
    a = jnp.moveaxis(a, 0, 2)
    return a.reshape(a.shape[:2] + (-1,) + a.shape[4:])


def rope(x, pos):
    half = x.shape[-1] // 2
    inv = ROPE_BASE ** (-jnp.arange(half, dtype=jnp.float32) / half)
    ang = pos[:, None] * inv[None, :]
    cos, sin = jnp.cos(ang), jnp.sin(ang)
    x1, x2 = x[..., :half], x[..., half:]
    return jnp.concatenate([x1 * cos - x2 * sin, x2 * cos + x1 * sin], axis=-1)


def mlstm_chunked(q, k, v, i_pre, log_f, c0, n0, m0):
    t = q.shape[2]
    length = math.gcd(t, CHUNK)
    nc = t // length
    tril = jnp.tril(jnp.ones((length, length), dtype=bool))

    def step(carry, xs):
        c, n, m = carry
        qc, kc, vc, ic, fc = xs
        b = jnp.cumsum(fc, axis=-1)
        dmat = jnp.where(tril, b[..., :, None] - b[..., None, :] + ic[..., None, :], -jnp.inf)
        inter = b + m[..., None]
        m_t = jnp.maximum(jnp.max(dmat, axis=-1), inter)
        w_intra = jnp.exp(dmat - m_t[..., None])
        w_inter = jnp.exp(inter - m_t)
        s = jnp.einsum('bhtd,bhsd->bhts', qc, kc) * w_intra
        num = jnp.einsum('bhts,bhsv->bhtv', s, vc) + w_inter[..., None] * jnp.einsum('bhtd,bhdv->bhtv', qc, c)
        den = jnp.sum(s, axis=-1) + w_inter * jnp.einsum('bhtd,bhd->bht', qc, n)
        h = num / jnp.maximum(jnp.abs(den), jnp.exp(-m_t))[..., None]
        b_last = b[..., -1]
        dec_s = b_last[..., None] - b + ic
        m_new = jnp.maximum(b_last + m, jnp.max(dec_s, axis=-1))
        w_s = jnp.exp(dec_s - m_new[..., None])
        scale = jnp.exp(b_last + m - m_new)
        c_new = scale[..., None, None] * c + jnp.einsum('bhs,bhsd,bhsv->bhdv', w_s, kc, vc)
        n_new = scale[..., None] * n + jnp.einsum('bhs,bhsd->bhd', w_s, kc)
        return (c_new, n_new, m_new), h

    xs = tuple(to_chunks(a, nc, length) for a in (q, k, v, i_pre, log_f))
    (c, n, m), hs = lax.scan(step, (c0, n0, m0), xs)
    return from_chunks(hs), c, n, m


def retention_chunked(q, k, v, s0):
    n_heads, t = q.shape[1], q.shape[2]
    length = math.gcd(t, CHUNK)
    nc = t // length
    log_g = jnp.log(1.0 - 2.0 ** (-5.0 - jnp.arange(n_heads, dtype=jnp.float32)))
    idx = jnp.arange(length, dtype=jnp.float32)
    rel = idx[:, None] - idx[None, :]
    dmask = jnp.where(rel >= 0, jnp.exp(log_g[:, None, None] * jnp.maximum(rel, 0.0)), 0.0)
    q_dec = jnp.exp(log_g[:, None] * (idx + 1.0))[:, :, None]
    k_dec = jnp.exp(log_g[:, None] * (length - 1.0 - idx))[:, :, None]
    g_len = jnp.exp(log_g * length)[:, None, None]

    def step(s_state, xs):
        qc, kc, vc = xs
        sc = jnp.einsum('bhtd,bhsd->bhts', qc, kc) * dmask
        o = jnp.einsum('bhts,bhsv->bhtv', sc, vc) + q_dec * jnp.einsum('bhtd,bhdv->bhtv', qc, s_state)
        s_new = g_len * s_state + jnp.einsum('bhsd,bhsv->bhdv', kc * k_dec, vc)
        return s_new, o

    xs = tuple(to_chunks(a, nc, length) for a in (q, k, v))
    s_fin, os_ = lax.scan(step, s0, xs)
    return from_chunks(os_), s_fin


def conv_mixer(h, buf_a, buf_b, w_in, a_w, b_w, b_bias, b_g, b_b, w_out):
    proj = h @ w_in
    a_bg, a_cg, a_x, a_z, b_val, b_gate, b_z = jnp.split(proj, 7, axis=-1)
    conv_a, new_a = causal_dwconv(a_cg * a_x, buf_a, a_w)
    y_a = a_bg * conv_a * jax.nn.silu(a_z)
    conv_b, new_b = causal_dwconv(b_val * jax.nn.sigmoid(b_gate), buf_b, b_w)
    y_b = jax.nn.silu(layernorm(conv_b + b_bias, b_g, b_b)) * jax.nn.silu(b_z)
    return jnp.concatenate([y_a, y_b], axis=-1) @ w_out, new_a, new_b


def recurrent_mixer(h, pos, c0, n0, m0, s0, w_in, c_i_b, c_f_b, c_hn_g, d_hn_g, w_out):
    f32 = jnp.float32
    proj = h @ w_in
    c_q, c_k, c_v, c_o, c_z, d_q, d_k, d_v, d_z = jnp.split(proj[..., :9 * BRANCH], 9, axis=-1)
    gates = proj[..., 9 * BRANCH:].astype(f32)
    i_pre = (gates[..., :N_HEADS_C] + c_i_b.astype(f32)).transpose(0, 2, 1)
    log_f = jax.nn.log_sigmoid(gates[..., N_HEADS_C:] + c_f_b.astype(f32)).transpose(0, 2, 1)
    scale = HEAD_DIM ** -0.5
    hc, c_new, n_new, m_new = mlstm_chunked(
        split_heads(c_q, N_HEADS_C).astype(f32), split_heads(c_k, N_HEADS_C).astype(f32) * scale,
        split_heads(c_v, N_HEADS_C).astype(f32), i_pre, log_f,
        c0.astype(f32), n0.astype(f32), m0.astype(f32))
    h_c = jax.nn.sigmoid(c_o.astype(f32)) * merge_heads(hc)
    y_c = headnorm(h_c, c_hn_g, N_HEADS_C) * jax.nn.silu(c_z.astype(f32))
    qd = rope(split_heads(d_q, N_HEADS_D).astype(f32), pos)
    kd = rope(split_heads(d_k, N_HEADS_D).astype(f32), pos) * scale
    hd, s_new = retention_chunked(qd, kd, split_heads(d_v, N_HEADS_D).astype(f32), s0.astype(f32))
    y_d = headnorm(merge_heads(hd), d_hn_g, N_HEADS_D) * jax.nn.silu(d_z.astype(f32))
    y = jnp.concatenate([y_c, y_d], axis=-1).astype(h.dtype) @ w_out
    return y, c_new, n_new, m_new, s_new


def setup_inputs(seed: int = 0) -> dict:
    key = jax.random.key(seed)
    ks = jax.random.split(key, 24)

    def nrm(k, shape, s):
        return jax.random.normal(k, shape, jnp.float32) * s

    return {
        'x_prompt': nrm(ks[0], (BATCH, SEQ, D_MODEL), 1.0),
        'x_sample': nrm(ks[1], (DEC_BATCH, DEC_SEQ, D_MODEL), 1.0),
        'state_a_conv': nrm(ks[2], (N_PAIRS, DEC_BATCH, A_CONV - 1, BRANCH), 1.0),
        'state_b_conv': nrm(ks[3], (N_PAIRS, DEC_BATCH, B_CONV - 1, BRANCH), 0.5),
        'state_c_C': nrm(ks[4], (N_PAIRS, DEC_BATCH, N_HEADS_C, HEAD_DIM, HEAD_DIM), 0.5),
        'state_c_n': nrm(ks[5], (N_PAIRS, DEC_BATCH, N_HEADS_C, HEAD_DIM), 0.5),
        'state_c_m': nrm(ks[6], (N_PAIRS, DEC_BATCH, N_HEADS_C), 1.0),
        'state_d_S': nrm(ks[7], (N_PAIRS, DEC_BATCH, N_HEADS_D, HEAD_DIM, HEAD_DIM), 0.5),
        'norm_pre': 1.0 + nrm(ks[8], (DEPTH, D_MODEL), 0.1),
        'norm_post': 1.0 + nrm(ks[9], (DEPTH, D_MODEL), 0.1),
        'w_in_conv': nrm(ks[10], (N_PAIRS, D_MODEL, CONV_IN_COLS), D_MODEL ** -0.5),
        'a_conv_w': nrm(ks[11], (N_PAIRS, A_CONV, BRANCH), A_CONV ** -0.5),
        'b_conv_w': nrm(ks[12], (N_PAIRS, B_CONV, BRANCH), B_CONV ** -0.5),
        'b_conv_b': nrm(ks[13], (N_PAIRS, BRANCH), 0.02),
        'b_ln_g': 1.0 + nrm(ks[14], (N_PAIRS, BRANCH), 0.1),
        'b_ln_b': nrm(ks[15], (N_PAIRS, BRANCH), 0.02),
        'w_out_conv': nrm(ks[16], (N_PAIRS, MIX_WIDTH, D_MODEL), MIX_WIDTH ** -0.5),
        'w_in_rec': nrm(ks[17], (N_PAIRS, D_MODEL, REC_IN_COLS), D_MODEL ** -0.5),
        'c_i_b': nrm(ks[18], (N_PAIRS, N_HEADS_C), 0.1),
        'c_f_b': jnp.linspace(3.0, 6.0, N_HEADS_C, dtype=jnp.float32)[None, :] + nrm(ks[19], (N_PAIRS, N_HEADS_C), 0.1),
        'c_hn_g': 1.0 + nrm(ks[20], (N_PAIRS, BRANCH), 0.1),
        'd_hn_g': 1.0 + nrm(ks[21], (N_PAIRS, BRANCH), 0.1),
        'w_out_rec': nrm(ks[22], (N_PAIRS, MIX_WIDTH, D_MODEL), MIX_WIDTH ** -0.5),
    }


def reference(x_prompt, x_sample, state_a_conv, state_b_conv, state_c_C, state_c_n, state_c_m, state_d_S,
              norm_pre, norm_post, w_in_conv, a_conv_w, b_conv_w, b_conv_b, b_ln_g, b_ln_b, w_out_conv,
              w_in_rec, c_i_b, c_f_b, c_hn_g, d_hn_g, w_out_rec):
    bp, tp, _ = x_prompt.shape
    ts = x_sample.shape[1]
    dt = x_prompt.dtype
    pos_p = jnp.arange(tp, dtype=jnp.float32)
    pos_s = PAST_LEN + jnp.arange(ts, dtype=jnp.float32)
    zero_a = jnp.zeros((bp, A_CONV - 1, BRANCH), dt)
    zero_b = jnp.zeros((bp, B_CONV - 1, BRANCH), dt)
    zero_c = jnp.zeros((bp, N_HEADS_C, HEAD_DIM, HEAD_DIM), jnp.float32)
    zero_n = jnp.zeros((bp, N_HEADS_C, HEAD_DIM), jnp.float32)
    zero_m = jnp.zeros((bp, N_HEADS_C), jnp.float32)
    zero_s = jnp.zeros((bp, N_HEADS_D, HEAD_DIM, HEAD_DIM), jnp.float32)

    a_p, a_s, b_p, b_s, cc_p, cc_s, cn_p, cn_s, cm_p, cm_s, ds_p, ds_s = ([] for _ in range(12))
    xp, xs = x_prompt, x_sample
    for layer in range(DEPTH):
        p = layer // 2
        hp = rmsnorm(xp, norm_pre[layer])
        hs = rmsnorm(xs, norm_pre[layer])
        if layer % 2 == 0:
            wts = (w_in_conv[p], a_conv_w[p], b_conv_w[p], b_conv_b[p], b_ln_g[p], b_ln_b[p], w_out_conv[p])
            out_p, na_p, nb_p = conv_mixer(hp, zero_a, zero_b, *wts)
            out_s, na_s, nb_s = conv_mixer(hs, state_a_conv[p], state_b_conv[p], *wts)
            a_p.append(na_p); a_s.append(na_s); b_p.append(nb_p); b_s.append(nb_s)
        else:
            wts = (w_in_rec[p], c_i_b[p], c_f_b[p], c_hn_g[p], d_hn_g[p], w_out_rec[p])
            out_p, c1, n1, m1, s1 = recurrent_mixer(hp, pos_p, zero_c, zero_n, zero_m, zero_s, *wts)
            out_s, c2, n2, m2, s2 = recurrent_mixer(hs, pos_s, state_c_C[p], state_c_n[p], state_c_m[p],
                                                    state_d_S[p], *wts)
            cc_p.append(c1); cn_p.append(n1); cm_p.append(m1); ds_p.append(s1)
            cc_s.append(c2); cn_s.append(n2); cm_s.append(m2); ds_s.append(s2)
        xp = xp + rmsnorm(out_p, norm_post[layer])
        xs = xs + rmsnorm(out_s, norm_post[layer])

    return (xp, xs,
            jnp.stack(a_p), jnp.stack(a_s), jnp.stack(b_p), jnp.stack(b_s),
            jnp.stack(cc_p), jnp.stack(cc_s), jnp.stack(cn_p), jnp.stack(cn_s),
            jnp.stack(cm_p), jnp.stack(cm_s), jnp.stack(ds_p), jnp.stack(ds_s))
```

```python
import functools
import math

import jax
import jax.numpy as jnp
from jax import lax
from jax.experimental import pallas as pl
from jax.experimental.pallas import tpu as pltpu

F32 = jnp.float32
BF16 = jnp.bfloat16

D_MODEL = 1024
BRANCH = 1024
A_CONV = 3
B_CONV = 31
N_HEADS = 4
HEAD_DIM = 256
CHUNK = 128
ROPE_BASE = 10000.0
PAST_LEN = 16384
EPS = 1e-6
GATE_LANES = 128
VMEM_LIMIT = 48 * 1024 * 1024

_ARB2 = pltpu.CompilerParams(dimension_semantics=("arbitrary", "arbitrary"), vmem_limit_bytes=VMEM_LIMIT)
_ARB1 = pltpu.CompilerParams(dimension_semantics=("arbitrary",), vmem_limit_bytes=VMEM_LIMIT)
_ARB3 = pltpu.CompilerParams(dimension_semantics=("arbitrary", "arbitrary", "arbitrary"), vmem_limit_bytes=VMEM_LIMIT)


def _silu(x):
    return x * jax.nn.sigmoid(x)


def _rms_scale(x):
    return lax.rsqrt(jnp.mean(x * x, axis=-1, keepdims=True) + EPS)


def _norm_matmul_body(x_ref, g_ref, w_ref, o_ref, h_scr):
    @pl.when(pl.program_id(1) == 0)
    def _():
        x = x_ref[...]
        h_scr[...] = (x * _rms_scale(x) * g_ref[...]).astype(BF16)

    o_ref[...] = jnp.dot(h_scr[...], w_ref[...], preferred_element_type=F32)


def norm_matmul(x2, g, w_bf16, tm, tn):
    n, d = x2.shape
    ncols = w_bf16.shape[1]
    return pl.pallas_call(
        _norm_matmul_body,
        grid=(n // tm, ncols // tn),
        in_specs=[pl.BlockSpec((tm, d), lambda i, j: (i, 0)),
                  pl.BlockSpec((1, d), lambda i, j: (0, 0)),
                  pl.BlockSpec((d, tn), lambda i, j: (0, j))],
        out_specs=pl.BlockSpec((tm, tn), lambda i, j: (i, j)),
        out_shape=jax.ShapeDtypeStruct((n, ncols), F32),
        scratch_shapes=[pltpu.VMEM((tm, d), BF16)],
        compiler_params=_ARB2,
        name="norm_matmul",
    )(x2, g.reshape(1, d), w_bf16)


def _conv_mix_body(p_ref, sa_ref, sb_ref, aw_ref, bw_ref, bb_ref, lg_ref, lb_ref,
                   y_ref, na_ref, nb_ref, xpa, xpb, cva, cvb, *, tt, tch):
    ti = pl.program_id(1)
    ha, hb = A_CONV - 1, B_CONV - 1

    @pl.when(ti == 0)
    def _():
        xpa[0:ha] = sa_ref[...]
        xpb[0:hb] = sb_ref[...]

    xpa[ha:ha + tt] = p_ref[:, :, 1 * BRANCH:2 * BRANCH] * p_ref[:, :, 2 * BRANCH:3 * BRANCH]
    xpb[hb:hb + tt] = p_ref[:, :, 4 * BRANCH:5 * BRANCH] * jax.nn.sigmoid(p_ref[:, :, 5 * BRANCH:6 * BRANCH])

    for c in range(BRANCH // 128):
        cs = slice(c * 128, (c + 1) * 128)
        for t0 in range(0, tt, tch):
            acc = xpa[t0:t0 + tch, :, cs] * aw_ref[0:1, cs]
            for j in range(1, A_CONV):
                acc = acc + xpa[t0 + j:t0 + j + tch, :, cs] * aw_ref[j:j + 1, cs]
            cva[t0:t0 + tch, :, cs] = acc
            acc = xpb[t0:t0 + tch, :, cs] * bw_ref[0:1, cs]
            for j in range(1, B_CONV):
                acc = acc + xpb[t0 + j:t0 + j + tch, :, cs] * bw_ref[j:j + 1, cs]
            cvb[t0:t0 + tch, :, cs] = acc

    for t0 in range(0, tt, tch):
        ts = slice(t0, t0 + tch)
        y_ref[ts, :, 0:BRANCH] = p_ref[ts, :, 0:BRANCH] * cva[ts] * _silu(p_ref[ts, :, 3 * BRANCH:4 * BRANCH])
        zb = cvb[ts] + bb_ref[...]
        mu = jnp.mean(zb, axis=-1, keepdims=True)
        zc = zb - mu
        var = jnp.mean(zc * zc, axis=-1, keepdims=True)
        ln = zc * lax.rsqrt(var + EPS) * lg_ref[...] + lb_ref[...]
        y_ref[ts, :, BRANCH:2 * BRANCH] = _silu(ln) * _silu(p_ref[ts, :, 6 * BRANCH:7 * BRANCH])

    @pl.when(ti == pl.num_programs(1) - 1)
    def _():
        na_ref[...] = xpa[tt:tt + ha]
        nb_ref[...] = xpb[tt:tt + hb]

    xpa[0:ha] = xpa[tt:tt + ha]
    xpb[0:hb] = xpb[tt:tt + hb]


def conv_mix(proj3, sa, sb, a_w, b_w, b_bias, ln_g, ln_b, tt, bb, tch):
    t, b, pc = proj3.shape
    ha, hb = A_CONV - 1, B_CONV - 1
    row = lambda v: v.reshape(1, BRANCH)
    const2 = lambda shape: pl.BlockSpec(shape, lambda bi, ti: (0, 0))
    return pl.pallas_call(
        functools.partial(_conv_mix_body, tt=tt, tch=tch),
        grid=(b // bb, t // tt),
        in_specs=[pl.BlockSpec((tt, bb, pc), lambda bi, ti: (ti, bi, 0)),
                  pl.BlockSpec((ha, bb, BRANCH), lambda bi, ti: (0, bi, 0)),
                  pl.BlockSpec((hb, bb, BRANCH), lambda bi, ti: (0, bi, 0)),
                  const2((A_CONV, BRANCH)), const2((B_CONV, BRANCH)),
                  const2((1, BRANCH)), const2((1, BRANCH)), const2((1, BRANCH))],
        out_specs=[pl.BlockSpec((tt, bb, 2 * BRANCH), lambda bi, ti: (ti, bi, 0)),
                   pl.BlockSpec((ha, bb, BRANCH), lambda bi, ti: (0, bi, 0)),
                   pl.BlockSpec((hb, bb, BRANCH), lambda bi, ti: (0, bi, 0))],
        out_shape=[jax.ShapeDtypeStruct((t, b, 2 * BRANCH), F32),
                   jax.ShapeDtypeStruct((ha, b, BRANCH), F32),
                   jax.ShapeDtypeStruct((hb, b, BRANCH), F32)],
        scratch_shapes=[pltpu.VMEM((ha + tt, bb, BRANCH), F32),
                        pltpu.VMEM((hb + tt, bb, BRANCH), F32),
                        pltpu.VMEM((tt, bb, BRANCH), F32),
                        pltpu.VMEM((tt, bb, BRANCH), F32)],
        compiler_params=_ARB2,
        name="conv_mix",
    )(proj3, sa, sb, a_w, b_w, row(b_bias), row(ln_g), row(ln_b))


def _out_proj_body(y_ref, w_ref, g_ref, x_ref, o_ref):
    z = jnp.dot(y_ref[...].astype(BF16), w_ref[...], preferred_element_type=F32)
    o_ref[...] = x_ref[...] + z * _rms_scale(z) * g_ref[...]


def out_proj(y2, w_bf16, g, x2, tm):
    n, k = y2.shape
    d = w_bf16.shape[1]
    return pl.pallas_call(
        _out_proj_body,
        grid=(n // tm,),
        in_specs=[pl.BlockSpec((tm, k), lambda i: (i, 0)),
                  pl.BlockSpec((k, d), lambda i: (0, 0)),
                  pl.BlockSpec((1, d), lambda i: (0, 0)),
                  pl.BlockSpec((tm, d), lambda i: (i, 0))],
        out_specs=pl.BlockSpec((tm, d), lambda i: (i, 0)),
        out_shape=jax.ShapeDtypeStruct((n, d), F32),
        compiler_params=_ARB1,
        name="out_proj",
    )(y2, w_bf16, g.reshape(1, d), x2)


def _gates_body(x_ref, g_ref, w_ref, b_ref, o_ref):
    x = x_ref[...]
    h = x * _rms_scale(x) * g_ref[...]
    pre = jnp.dot(h, w_ref[...], preferred_element_type=F32, precision=lax.Precision.HIGHEST) + b_ref[...]
    log_f = jnp.minimum(pre, 0.0) - jnp.log1p(jnp.exp(-jnp.abs(pre)))
    lane = lax.broadcasted_iota(jnp.int32, pre.shape, 1)
    o_ref[...] = jnp.where(lane < N_HEADS, pre, log_f)


def gates(x2, g, w_gate, bias, tm):
    n, d = x2.shape
    return pl.pallas_call(
        _gates_body,
        grid=(n // tm,),
        in_specs=[pl.BlockSpec((tm, d), lambda i: (i, 0)),
                  pl.BlockSpec((1, d), lambda i: (0, 0)),
                  pl.BlockSpec((d, GATE_LANES), lambda i: (0, 0)),
                  pl.BlockSpec((1, GATE_LANES), lambda i: (0, 0))],
        out_specs=pl.BlockSpec((tm, GATE_LANES), lambda i: (i, 0)),
        out_shape=jax.ShapeDtypeStruct((n, GATE_LANES), F32),
        compiler_params=_ARB1,
        name="gates",
    )(x2, g.reshape(1, d), w_gate, bias)


_NT = (((1,), (1,)), ((), ()))
_TN = (((0,), (0,)), ((), ()))


def _mlstm_body(q_ref, k_ref, v_ref, gt_ref, c0_ref, n0_ref, m0_ref,
                h_ref, c_ref, n_ref, m_ref, *, length):
    hidx = pl.program_id(1)

    @pl.when(pl.program_id(2) == 0)
    def _():
        c_ref[...] = c0_ref[...]
        n_ref[...] = n0_ref[...]
        m_ref[...] = m0_ref[...]

    ll = (length, length)
    g = gt_ref[...]
    lane = lax.broadcasted_iota(jnp.int32, g.shape, 1)
    icol = jnp.sum(jnp.where(lane == hidx, g, 0.0), axis=-1, keepdims=True)
    fcol = jnp.sum(jnp.where(lane == hidx + N_HEADS, g, 0.0), axis=-1, keepdims=True)
    r = lax.broadcasted_iota(jnp.int32, ll, 0)
    s = lax.broadcasted_iota(jnp.int32, ll, 1)
    eye, tril = r == s, r >= s
    frow = jnp.sum(jnp.where(eye, fcol, 0.0), axis=0, keepdims=True)
    irow = jnp.sum(jnp.where(eye, icol, 0.0), axis=0, keepdims=True)
    b_col = jnp.sum(jnp.where(tril, frow, 0.0), axis=-1, keepdims=True)
    b_row = jnp.sum(jnp.where(r <= s, fcol, 0.0), axis=0, keepdims=True)
    m = m_ref[:, 0:1]

    dmat = jnp.where(tril, b_col - b_row + irow, -jnp.inf)
    inter = b_col + m
    m_t = jnp.maximum(jnp.max(dmat, axis=-1, keepdims=True), inter)
    w_intra = jnp.exp(dmat - m_t)
    w_inter = jnp.exp(inter - m_t)

    scale_k = HEAD_DIM ** -0.5
    q, k, v = q_ref[...], k_ref[...], v_ref[...]
    qb, kb, vb = q.astype(BF16), k.astype(BF16), v.astype(BF16)
    c_state, n_state = c_ref[...], n_ref[...]
    sc = lax.dot_general(qb, kb, _NT, preferred_element_type=F32) * (w_intra * scale_k)
    num = (jnp.dot(sc.astype(BF16), vb, preferred_element_type=F32)
           + w_inter * jnp.dot(qb, c_state.astype(BF16), preferred_element_type=F32))
    den = jnp.sum(sc, axis=-1, keepdims=True) + w_inter * jnp.sum(q * n_state, axis=-1, keepdims=True)
    h_ref[...] = num / jnp.maximum(jnp.abs(den), jnp.exp(-m_t))

    b_last = jnp.sum(frow, axis=-1, keepdims=True)
    dec = b_last - b_col + icol
    m_new = jnp.maximum(b_last + m, jnp.max(dec, axis=0, keepdims=True))
    w_s = jnp.exp(dec - m_new)
    decay = jnp.exp(b_last + m - m_new)
    kw = (k * scale_k) * w_s
    c_ref[...] = decay * c_state + lax.dot_general(kw.astype(BF16), vb, _TN, preferred_element_type=F32)
    n_ref[...] = decay * n_state + jnp.sum(kw, axis=0, keepdims=True)
    m_ref[...] = jnp.broadcast_to(m_new, m_ref.shape)


def _state_spec(shape):
    return pl.BlockSpec((None, None) + shape, lambda b, h, c: (b, h, 0, 0))


def mlstm(proj3, gates3, c0, n0, m0, length):
    bsz, t, _ = proj3.shape
    hd = HEAD_DIM
    tok = lambda off: pl.BlockSpec((None, length, hd), lambda b, h, c: (b, c, off + h))
    return pl.pallas_call(
        functools.partial(_mlstm_body, length=length),
        grid=(bsz, N_HEADS, t // length),
        in_specs=[tok(0), tok(N_HEADS), tok(2 * N_HEADS),
                  pl.BlockSpec((None, length, GATE_LANES), lambda b, h, c: (b, c, 0)),
                  _state_spec((hd, hd)), _state_spec((1, hd)), _state_spec((1, GATE_LANES))],
        out_specs=[tok(0), _state_spec((hd, hd)), _state_spec((1, hd)), _state_spec((1, GATE_LANES))],
        out_shape=[jax.ShapeDtypeStruct((bsz, t, BRANCH), F32),
                   jax.ShapeDtypeStruct((bsz, N_HEADS, hd, hd), F32),
                   jax.ShapeDtypeStruct((bsz, N_HEADS, 1, hd), F32),
                   jax.ShapeDtypeStruct((bsz, N_HEADS, 1, GATE_LANES), F32)],
        compiler_params=_ARB3,
        name="mlstm",
    )(proj3, proj3, proj3, gates3, c0, n0, m0)


def _rope(x, cos, sin):
    half = HEAD_DIM // 2
    x1, x2 = x[:, :half], x[:, half:]
    return jnp.concatenate([x1 * cos - x2 * sin, x2 * cos + x1 * sin], axis=-1)


def _retention_body(q_ref, k_ref, v_ref, cos_ref, sin_ref, lg_ref, s0_ref, o_ref, s_ref, *, length):
    @pl.when(pl.program_id(2) == 0)
    def _():
        s_ref[...] = s0_ref[...]

    ll = (length, length)
    log_g = lg_ref[:, 0:1]
    rel = (lax.broadcasted_iota(jnp.int32, ll, 0) - lax.broadcasted_iota(jnp.int32, ll, 1)).astype(F32)
    dmask = jnp.where(rel >= 0, jnp.exp(log_g * jnp.maximum(rel, 0.0)), 0.0)
    idx = lax.broadcasted_iota(jnp.int32, (length, 1), 0).astype(F32)
    q_dec = jnp.exp(log_g * (idx + 1.0))
    k_dec = jnp.exp(log_g * (length - 1.0 - idx))
    g_len = jnp.exp(log_g * length)

    scale_k = HEAD_DIM ** -0.5
    cos, sin = cos_ref[...], sin_ref[...]
    q = _rope(q_ref[...], cos, sin)
    k = _rope(k_ref[...], cos, sin) * scale_k
    qb, kb, vb = q.astype(BF16), k.astype(BF16), v_ref[...].astype(BF16)
    s_state = s_ref[...]
    sc = lax.dot_general(qb, kb, _NT, preferred_element_type=F32) * dmask
    o_ref[...] = (jnp.dot(sc.astype(BF16), vb, preferred_element_type=F32)
                  + q_dec * jnp.dot(qb, s_state.astype(BF16), preferred_element_type=F32))
    s_ref[...] = g_len * s_state + lax.dot_general((k * k_dec).astype(BF16), vb, _TN, preferred_element_type=F32)


def retention(proj3, cos, sin, log_g, s0, length):
    bsz, t, _ = proj3.shape
    hd = HEAD_DIM
    tok = lambda off: pl.BlockSpec((None, length, hd), lambda b, h, c: (b, c, off + h))
    rot = pl.BlockSpec((length, hd // 2), lambda b, h, c: (c, 0))
    return pl.pallas_call(
        functools.partial(_retention_body, length=length),
        grid=(bsz, N_HEADS, t // length),
        in_specs=[tok(5 * N_HEADS), tok(6 * N_HEADS), tok(7 * N_HEADS), rot, rot,
                  pl.BlockSpec((None, 1, GATE_LANES), lambda b, h, c: (h, 0, 0)),
                  _state_spec((hd, hd))],
        out_specs=[tok(0), _state_spec((hd, hd))],
        out_shape=[jax.ShapeDtypeStruct((bsz, t, BRANCH), F32),
                   jax.ShapeDtypeStruct((bsz, N_HEADS, hd, hd), F32)],
        compiler_params=_ARB3,
        name="retention",
    )(proj3, proj3, proj3, cos, sin, log_g, s0)


def _headnorm(x, g):
    parts = []
    for h in range(N_HEADS):
        xs = x[:, h * HEAD_DIM:(h + 1) * HEAD_DIM]
        mu = jnp.mean(xs, axis=-1, keepdims=True)
        xc = xs - mu
        var = jnp.mean(xc * xc, axis=-1, keepdims=True)
        parts.append(xc * lax.rsqrt(var + EPS))
    return jnp.concatenate(parts, axis=-1) * g


def _rec_out_body(hc_ref, hd_ref, co_ref, cz_ref, dz_ref, gc_ref, gd_ref, w_ref, g_ref, x_ref, o_ref):
    y_c = _headnorm(jax.nn.sigmoid(co_ref[...]) * hc_ref[...], gc_ref[...]) * _silu(cz_ref[...])
    y_d = _headnorm(hd_ref[...], gd_ref[...]) * _silu(dz_ref[...])
    z = (jnp.dot(y_c.astype(BF16), w_ref[0:BRANCH, :], preferred_element_type=F32)
         + jnp.dot(y_d.astype(BF16), w_ref[BRANCH:2 * BRANCH, :], preferred_element_type=F32))
    o_ref[...] = x_ref[...] + z * _rms_scale(z) * g_ref[...]


def rec_out(hc2, hd2, proj2, g_c, g_d, w_bf16, g_post, x2, tm):
    n, d = x2.shape
    rows = lambda col: pl.BlockSpec((tm, BRANCH), lambda i: (i, col))
    vec = pl.BlockSpec((1, BRANCH), lambda i: (0, 0))
    return pl.pallas_call(
        _rec_out_body,
        grid=(n // tm,),
        in_specs=[rows(0), rows(0), rows(3), rows(4), rows(8), vec, vec,
                  pl.BlockSpec((2 * BRANCH, d), lambda i: (0, 0)), vec, rows(0)],
        out_specs=rows(0),
        out_shape=jax.ShapeDtypeStruct((n, d), F32),
        compiler_params=_ARB1,
        name="rec_out",
    )(hc2, hd2, proj2, proj2, proj2, g_c.reshape(1, -1), g_d.reshape(1, -1), w_bf16, g_post.reshape(1, -1), x2)


def _conv_layer(x_tm, sa, sb, g_pre, g_post, w_in, a_w, b_w, b_bias, ln_g, ln_b, w_out, tt, bb, tch):
    t, b, d = x_tm.shape
    x2 = x_tm.reshape(t * b, d)
    proj = norm_matmul(x2, g_pre, w_in, tm=1024, tn=1024)
    y, na, nb = conv_mix(proj.reshape(t, b, -1), sa, sb, a_w, b_w, b_bias, ln_g, ln_b, tt, bb, tch)
    x_out = out_proj(y.reshape(t * b, -1), w_out, g_post, x2, tm=512)
    return x_out.reshape(t, b, d), na, nb


def _rec_layer(x_bm, pos, c0, n0, m0, s0, g_pre, g_post, w_in, w_gate, gate_bias, g_c, g_d, w_out):
    b, t, d = x_bm.shape
    length = math.gcd(t, CHUNK)
    x2 = x_bm.reshape(b * t, d)
    proj = norm_matmul(x2, g_pre, w_in, tm=1024, tn=1024)
    gt = gates(x2, g_pre, w_gate, gate_bias, tm=1024)
    proj3 = proj.reshape(b, t, -1)
    m0b = jnp.broadcast_to(m0[:, :, None, None], (b, N_HEADS, 1, GATE_LANES))
    hc, c_new, n_new, m_new = mlstm(proj3, gt.reshape(b, t, GATE_LANES), c0, n0[:, :, None, :], m0b, length)
    half = HEAD_DIM // 2
    inv = ROPE_BASE ** (-jnp.arange(half, dtype=F32) / half)
    ang = pos[:, None] * inv[None, :]
    log_g = jnp.log(1.0 - 2.0 ** (-5.0 - jnp.arange(N_HEADS, dtype=F32)))
    log_g = jnp.broadcast_to(log_g[:, None, None], (N_HEADS, 1, GATE_LANES))
    hd, s_new = retention(proj3, jnp.cos(ang), jnp.sin(ang), log_g, s0, length)
    x_out = rec_out(hc.reshape(b * t, -1), hd.reshape(b * t, -1), proj, g_c, g_d, w_out, g_post, x2, tm=512)
    return x_out.reshape(b, t, d), c_new, n_new[:, :, 0, :], m_new[:, :, 0, 0], s_new


def kernel(x_prompt, x_sample, state_a_conv, state_b_conv, state_c_C, state_c_n, state_c_m, state_d_S, norm_pre, norm_post, w_in_conv, a_conv_w, b_conv_w, b_conv_b, b_ln_g, b_ln_b, w_out_conv, w_in_rec, c_i_b, c_f_b, c_hn_g, d_hn_g, w_out_rec):
    bp, tp, _ = x_prompt.shape
    bs, ts, _ = x_sample.shape
    depth = norm_pre.shape[0]
    pos_p = jnp.arange(tp, dtype=F32)
    pos_s = PAST_LEN + jnp.arange(ts, dtype=F32)
    tm_axes = (1, 0, 2)

    outs = [[] for _ in range(12)]
    xp, xs = x_prompt, x_sample
    for layer in range(depth):
        p = layer // 2
        if layer % 2 == 0:
            wts = (norm_pre[layer], norm_post[layer], w_in_conv[p].astype(BF16), a_conv_w[p], b_conv_w[p],
                   b_conv_b[p], b_ln_g[p], b_ln_b[p], w_out_conv[p].astype(BF16))
            zero_a = jnp.zeros((A_CONV - 1, bp, BRANCH), F32)
            zero_b = jnp.zeros((B_CONV - 1, bp, BRANCH), F32)
            op, na_p, nb_p = _conv_layer(xp.transpose(tm_axes), zero_a, zero_b, *wts, tt=32, bb=8, tch=16)
            os_, na_s, nb_s = _conv_layer(xs.transpose(tm_axes), state_a_conv[p].transpose(tm_axes),
                                          state_b_conv[p].transpose(tm_axes), *wts, tt=ts, bb=16, tch=8)
            xp, xs = op.transpose(tm_axes), os_.transpose(tm_axes)
            new = (na_p.transpose(tm_axes), na_s.transpose(tm_axes), nb_p.transpose(tm_axes), nb_s.transpose(tm_axes))
            for lst, val in zip(outs[0:4], new):
                lst.append(val)
        else:
            n_main = 9 * BRANCH
            w_gate = jnp.pad(w_in_rec[p][:, n_main:], ((0, 0), (0, GATE_LANES - 2 * N_HEADS)))
            gate_bias = jnp.pad(jnp.concatenate([c_i_b[p], c_f_b[p]]), (0, GATE_LANES - 2 * N_HEADS)).reshape(1, GATE_LANES)
            wts = (norm_pre[layer], norm_post[layer], w_in_rec[p][:, :n_main].astype(BF16), w_gate, gate_bias,
                   c_hn_g[p], d_hn_g[p], w_out_rec[p].astype(BF16))
            zc = jnp.zeros((bp, N_HEADS, HEAD_DIM, HEAD_DIM), F32)
            zn = jnp.zeros((bp, N_HEADS, HEAD_DIM), F32)
            zm = jnp.zeros((bp, N_HEADS), F32)
            xp, c1, n1, m1, s1 = _rec_layer(xp, pos_p, zc, zn, zm, zc, *wts)
            xs, c2, n2, m2, s2 = _rec_layer(xs, pos_s, state_c_C[p], state_c_n[p], state_c_m[p], state_d_S[p], *wts)
            for lst, val in zip(outs[4:12], (c1, c2, n1, n2, m1, m2, s1, s2)):
                lst.append(val)

    return (xp, xs) + tuple(jnp.stack(lst) for lst in outs)
```

```python
import functools
import math

import jax
import jax.numpy as jnp
from jax import lax
from jax.experimental import pallas as pl
from jax.experimental.pallas import tpu as pltpu

F32 = jnp.float32
BF16 = jnp.bfloat16

D_MODEL = 1024
BRANCH = 1024
A_CONV = 3
B_CONV = 31
N_HEADS = 4
HEAD_DIM = 256
CHUNK = 128
ROPE_BASE = 10000.0
PAST_LEN = 16384
EPS = 1e-6
GATE_LANES = 128
VMEM_LIMIT = 48 * 1024 * 1024

_ARB2 = pltpu.CompilerParams(dimension_semantics=("arbitrary", "arbitrary"), vmem_limit_bytes=VMEM_LIMIT)
_ARB1 = pltpu.CompilerParams(dimension_semantics=("arbitrary",), vmem_limit_bytes=VMEM_LIMIT)


def _silu(x):
    return x * jax.nn.sigmoid(x)


def _rms_scale(x):
    return lax.rsqrt(jnp.mean(x * x, axis=-1, keepdims=True) + EPS)


def _norm_matmul_body(x_ref, g_ref, w_ref, o_ref, h_scr):
    @pl.when(pl.program_id(1) == 0)
    def _():
        x = x_ref[...]
        h_scr[...] = (x * _rms_scale(x) * g_ref[...]).astype(BF16)

    o_ref[...] = jnp.dot(h_scr[...], w_ref[...], preferred_element_type=F32)


def norm_matmul(x2, g, w_bf16, tm, tn):
    n, d = x2.shape
    ncols = w_bf16.shape[1]
    return pl.pallas_call(
        _norm_matmul_body,
        grid=(n // tm, ncols // tn),
        in_specs=[pl.BlockSpec((tm, d), lambda i, j: (i, 0)),
                  pl.BlockSpec((1, d), lambda i, j: (0, 0)),
                  pl.BlockSpec((d, tn), lambda i, j: (0, j))],
        out_specs=pl.BlockSpec((tm, tn), lambda i, j: (i, j)),
        out_shape=jax.ShapeDtypeStruct((n, ncols), F32),
        scratch_shapes=[pltpu.VMEM((tm, d), BF16)],
        compiler_params=_ARB2,
        name="norm_matmul",
    )(x2, g.reshape(1, d), w_bf16)


def _conv_mix_body(p_ref, sa_ref, sb_ref, aw_ref, bw_ref, bb_ref, lg_ref, lb_ref,
                   y_ref, na_ref, nb_ref, xpa, xpb, cva, cvb, *, tt, tch):
    ti = pl.program_id(1)
    ha, hb = A_CONV - 1, B_CONV - 1

    @pl.when(ti == 0)
    def _():
        xpa[0:ha] = sa_ref[...]
        xpb[0:hb] = sb_ref[...]

    xpa[ha:ha + tt] = p_ref[:, :, 1 * BRANCH:2 * BRANCH] * p_ref[:, :, 2 * BRANCH:3 * BRANCH]
    xpb[hb:hb + tt] = p_ref[:, :, 4 * BRANCH:5 * BRANCH] * jax.nn.sigmoid(p_ref[:, :, 5 * BRANCH:6 * BRANCH])

    for c in range(BRANCH // 128):
        cs = slice(c * 128, (c + 1) * 128)
        for t0 in range(0, tt, tch):
            acc = xpa[t0:t0 + tch, :, cs] * aw_ref[0:1, cs]
            for j in range(1, A_CONV):
                acc = acc + xpa[t0 + j:t0 + j + tch, :, cs] * aw_ref[j:j + 1, cs]
            cva[t0:t0 + tch, :, cs] = acc
            acc = xpb[t0:t0 + tch, :, cs] * bw_ref[0:1, cs]
            for j in range(1, B_CONV):
                acc = acc + xpb[t0 + j:t0 + j + tch, :, cs] * bw_ref[j:j + 1, cs]
            cvb[t0:t0 + tch, :, cs] = acc

    for t0 in range(0, tt, tch):
        ts = slice(t0, t0 + tch)
        y_ref[ts, :, 0:BRANCH] = p_ref[ts, :, 0:BRANCH] * cva[ts] * _silu(p_ref[ts, :, 3 * BRANCH:4 * BRANCH])
        zb = cvb[ts] + bb_ref[...]
        mu = jnp.mean(zb, axis=-1, keepdims=True)
        zc = zb - mu
        var = jnp.mean(zc * zc, axis=-1, keepdims=True)
        ln = zc * lax.rsqrt(var + EPS) * lg_ref[...] + lb_ref[...]
        y_ref[ts, :, BRANCH:2 * BRANCH] = _silu(ln) * _silu(p_ref[ts, :, 6 * BRANCH:7 * BRANCH])

    @pl.when(ti == pl.num_programs(1) - 1)
    def _():
        na_ref[...] = xpa[tt:tt + ha]
        nb_ref[...] = xpb[tt:tt + hb]

    xpa[0:ha] = xpa[tt:tt + ha]
    xpb[0:hb] = xpb[tt:tt + hb]


def conv_mix(proj3, sa, sb, a_w, b_w, b_bias, ln_g, ln_b, tt, bb, tch):
    t, b, pc = proj3.shape
    ha, hb = A_CONV - 1, B_CONV - 1
    row = lambda v: v.reshape(1, BRANCH)
    const2 = lambda shape: pl.BlockSpec(shape, lambda bi, ti: (0, 0))
    return pl.pallas_call(
        functools.partial(_conv_mix_body, tt=tt, tch=tch),
        grid=(b // bb, t // tt),
        in_specs=[pl.BlockSpec((tt, bb, pc), lambda bi, ti: (ti, bi, 0)),
                  pl.BlockSpec((ha, bb, BRANCH), lambda bi, ti: (0, bi, 0)),
                  pl.BlockSpec((hb, bb, BRANCH), lambda bi, ti: (0, bi, 0)),
                  const2((A_CONV, BRANCH)), const2((B_CONV, BRANCH)),
                  const2((1, BRANCH)), const2((1, BRANCH)), const2((1, BRANCH))],
        out_specs=[pl.BlockSpec((tt, bb, 2 * BRANCH), lambda bi, ti: (ti, bi, 0)),
                   pl.BlockSpec((ha, bb, BRANCH), lambda bi, ti: (0, bi, 0)),
                   pl.BlockSpec((hb, bb, BRANCH), lambda bi, ti: (0, bi, 0))],
        out_shape=[jax.ShapeDtypeStruct((t, b, 2 * BRANCH), F32),
                   jax.ShapeDtypeStruct((ha, b, BRANCH), F32),
                   jax.ShapeDtypeStruct((hb, b, BRANCH), F32)],
        scratch_shapes=[pltpu.VMEM((ha + tt, bb, BRANCH), F32),
                        pltpu.VMEM((hb + tt, bb, BRANCH), F32),
                        pltpu.VMEM((tt, bb, BRANCH), F32),
                        pltpu.VMEM((tt, bb, BRANCH), F32)],
        compiler_params=_ARB2,
        name="conv_mix",
    )(proj3, sa, sb, a_w, b_w, row(b_bias), row(ln_g), row(ln_b))


def _out_proj_body(y_ref, w_ref, g_ref, x_ref, o_ref):
    z = jnp.dot(y_ref[...].astype(BF16), w_ref[...], preferred_element_type=F32)
    o_ref[...] = x_ref[...] + z * _rms_scale(z) * g_ref[...]


def out_proj(y2, w_bf16, g, x2, tm):
    n, k = y2.shape
    d = w_bf16.shape[1]
    return pl.pallas_call(
        _out_proj_body,
        grid=(n // tm,),
        in_specs=[pl.BlockSpec((tm, k), lambda i: (i, 0)),
                  pl.BlockSpec((k, d), lambda i: (0, 0)),
                  pl.BlockSpec((1, d), lambda i: (0, 0)),
                  pl.BlockSpec((tm, d), lambda i: (i, 0))],
        out_specs=pl.BlockSpec((tm, d), lambda i: (i, 0)),
        out_shape=jax.ShapeDtypeStruct((n, d), F32),
        compiler_params=_ARB1,
        name="out_proj",
    )(y2, w_bf16, g.reshape(1, d), x2)


REC_COL_ORDER = (0, 1, 2, 5, 6, 7, 3, 4, 8)
N_OPERAND_BLOCKS = 6
N_GATING_BLOCKS = 3
ROPE_BLOCKS = (3, 4)


def _rec_in_proj_body(x_ref, g_ref, w_ref, wg_ref, gb_ref, cos_ref, sin_ref, ob_ref, of_ref, gt_ref, h_scr):
    j = pl.program_id(1)

    @pl.when(j == 0)
    def _():
        x = x_ref[...]
        h = x * _rms_scale(x) * g_ref[...]
        h_scr[...] = h.astype(BF16)
        pre = jnp.dot(h, wg_ref[...], preferred_element_type=F32, precision=lax.Precision.HIGHEST) + gb_ref[...]
        log_f = jnp.minimum(pre, 0.0) - jnp.log1p(jnp.exp(-jnp.abs(pre)))
        lane = lax.broadcasted_iota(jnp.int32, pre.shape, 1)
        gt_ref[...] = jnp.where(lane < N_HEADS, pre, log_f)

    z = jnp.dot(h_scr[...], w_ref[...], preferred_element_type=F32)
    is_rope = (j == ROPE_BLOCKS[0]) | (j == ROPE_BLOCKS[1])

    @pl.when((j < N_OPERAND_BLOCKS) & jnp.logical_not(is_rope))
    def _():
        ob_ref[...] = z.astype(BF16)

    @pl.when(is_rope)
    def _():
        cos, sin = cos_ref[...], sin_ref[...]
        half = HEAD_DIM // 2
        for h in range(N_HEADS):
            lo, mid, hi = h * HEAD_DIM, h * HEAD_DIM + half, (h + 1) * HEAD_DIM
            x1, x2 = z[:, lo:mid], z[:, mid:hi]
            ob_ref[:, lo:mid] = (x1 * cos - x2 * sin).astype(BF16)
            ob_ref[:, mid:hi] = (x2 * cos + x1 * sin).astype(BF16)

    @pl.when(j >= N_OPERAND_BLOCKS)
    def _():
        of_ref[...] = z


def rec_in_proj(x2, g, w_bf16, w_gate, gate_bias, cos_rows, sin_rows, tm):
    n, d = x2.shape
    tn = BRANCH
    n_period = cos_rows.shape[0] // tm
    const = lambda shape: pl.BlockSpec(shape, lambda i, j: (0, 0))
    rot = pl.BlockSpec((tm, HEAD_DIM // 2), lambda i, j: (i % n_period, 0))
    return pl.pallas_call(
        _rec_in_proj_body,
        grid=(n // tm, N_OPERAND_BLOCKS + N_GATING_BLOCKS),
        in_specs=[pl.BlockSpec((tm, d), lambda i, j: (i, 0)), const((1, d)),
                  pl.BlockSpec((d, tn), lambda i, j: (0, j)),
                  const((d, GATE_LANES)), const((1, GATE_LANES)), rot, rot],
        out_specs=[pl.BlockSpec((tm, tn), lambda i, j: (i, jnp.minimum(j, N_OPERAND_BLOCKS - 1))),
                   pl.BlockSpec((tm, tn), lambda i, j: (i, jnp.maximum(j - N_OPERAND_BLOCKS, 0))),
                   pl.BlockSpec((tm, GATE_LANES), lambda i, j: (i, 0))],
        out_shape=[jax.ShapeDtypeStruct((n, N_OPERAND_BLOCKS * tn), BF16),
                   jax.ShapeDtypeStruct((n, N_GATING_BLOCKS * tn), F32),
                   jax.ShapeDtypeStruct((n, GATE_LANES), F32)],
        scratch_shapes=[pltpu.VMEM((tm, d), BF16)],
        compiler_params=_ARB2,
        name="rec_in_proj",
    )(x2, g.reshape(1, d), w_bf16, w_gate, gate_bias, cos_rows, sin_rows)


_NT = (((1,), (1,)), ((), ()))
_TN = (((0,), (0,)), ((), ()))
SCALE_K = HEAD_DIM ** -0.5


def _mlstm_chunk(q, k, v, icol, fcol, c_state, n_state, m, r, s):
    eye, tril = r == s, r >= s
    frow = jnp.sum(jnp.where(eye, fcol, 0.0), axis=0, keepdims=True)
    irow = jnp.sum(jnp.where(eye, icol, 0.0), axis=0, keepdims=True)
    b_col = jnp.sum(jnp.where(tril, frow, 0.0), axis=-1, keepdims=True)
    b_row = jnp.sum(jnp.where(r <= s, fcol, 0.0), axis=0, keepdims=True)

    dmat = jnp.where(tril, b_col - b_row + irow, -jnp.inf)
    inter = b_col + m
    m_t = jnp.maximum(jnp.max(dmat, axis=-1, keepdims=True), inter)
    w_intra = jnp.exp(dmat - m_t)
    w_inter = jnp.exp(inter - m_t)

    sc = lax.dot_general(q, k, _NT, preferred_element_type=F32) * (w_intra * SCALE_K)
    num = (jnp.dot(sc.astype(BF16), v, preferred_element_type=F32)
           + w_inter * jnp.dot(q, c_state.astype(BF16), preferred_element_type=F32))
    den = (jnp.sum(sc, axis=-1, keepdims=True)
           + w_inter * jnp.sum(q.astype(F32) * n_state, axis=-1, keepdims=True))
    h = num / jnp.maximum(jnp.abs(den), jnp.exp(-m_t))

    b_last = jnp.sum(frow, axis=-1, keepdims=True)
    dec = b_last - b_col + icol
    m_new = jnp.maximum(b_last + m, jnp.max(dec, axis=0, keepdims=True))
    w_s = jnp.exp(dec - m_new)
    decay = jnp.exp(b_last + m - m_new)
    kw = (k.astype(F32) * SCALE_K) * w_s
    c_new = decay * c_state + lax.dot_general(kw.astype(BF16), v, _TN, preferred_element_type=F32)
    n_new = decay * n_state + jnp.sum(kw, axis=0, keepdims=True)
    return h, c_new, n_new, m_new


def _retention_chunk(q, k, v, s_state, log_g, r, s, length):
    rel = (r - s).astype(F32)
    dmask = jnp.where(rel >= 0, jnp.exp(log_g * jnp.maximum(rel, 0.0)), 0.0)
    idx = lax.broadcasted_iota(jnp.int32, (length, 1), 0).astype(F32)
    q_dec = jnp.exp(log_g * (idx + 1.0))
    k_dec = jnp.exp(log_g * (length - 1.0 - idx))
    g_len = math.exp(log_g * length)
    sc = lax.dot_general(q, k, _NT, preferred_element_type=F32) * (dmask * SCALE_K)
    o = (jnp.dot(sc.astype(BF16), v, preferred_element_type=F32)
         + q_dec * jnp.dot(q, s_state.astype(BF16), preferred_element_type=F32))
    kd = (k.astype(F32) * SCALE_K) * k_dec
    s_new = g_len * s_state + lax.dot_general(kd.astype(BF16), v, _TN, preferred_element_type=F32)
    return o, s_new


def _norm_rows(x):
    mu = jnp.mean(x, axis=-1, keepdims=True)
    xc = x - mu
    return xc * lax.rsqrt(jnp.mean(xc * xc, axis=-1, keepdims=True) + EPS)


def _rec_mix_body(qkv_ref, gat_ref, gt_ref, gc_ref, gd_ref, c0_ref, n0_ref, m0_ref, s0_ref,
                  y_ref, c_ref, n_ref, m_ref, s_ref, *, nseq, length):
    @pl.when(pl.program_id(1) == 0)
    def _():
        c_ref[...] = c0_ref[...]
        n_ref[...] = n0_ref[...]
        m_ref[...] = m0_ref[...]
        s_ref[...] = s0_ref[...]

    r = lax.broadcasted_iota(jnp.int32, (length, length), 0)
    s = lax.broadcasted_iota(jnp.int32, (length, length), 1)
    lane = lax.broadcasted_iota(jnp.int32, (length, GATE_LANES), 1)
    col = lambda blk, h: slice(blk * BRANCH + h * HEAD_DIM, blk * BRANCH + (h + 1) * HEAD_DIM)
    for sq in range(nseq):
        g = gt_ref[sq]
        for h in range(N_HEADS):
            icol = jnp.sum(jnp.where(lane == h, g, 0.0), axis=-1, keepdims=True)
            fcol = jnp.sum(jnp.where(lane == h + N_HEADS, g, 0.0), axis=-1, keepdims=True)
            hc, c_new, n_new, m_new = _mlstm_chunk(
                qkv_ref[sq, :, col(0, h)], qkv_ref[sq, :, col(1, h)], qkv_ref[sq, :, col(2, h)],
                icol, fcol, c_ref[sq, h], n_ref[sq, h], m_ref[sq, h][:, 0:1], r, s)
            c_ref[sq, h] = c_new
            n_ref[sq, h] = n_new
            m_ref[sq, h] = jnp.broadcast_to(m_new, (1, GATE_LANES))
            y_c = (_norm_rows(jax.nn.sigmoid(gat_ref[sq, :, col(0, h)]) * hc) * gc_ref[:, col(0, h)]
                   * _silu(gat_ref[sq, :, col(1, h)]))
            y_ref[sq, :, col(0, h)] = y_c.astype(BF16)

            log_g = math.log(1.0 - 2.0 ** (-5.0 - h))
            o, s_new = _retention_chunk(
                qkv_ref[sq, :, col(3, h)], qkv_ref[sq, :, col(4, h)], qkv_ref[sq, :, col(5, h)],
                s_ref[sq, h], log_g, r, s, length)
            s_ref[sq, h] = s_new
            y_d = _norm_rows(o) * gd_ref[:, col(0, h)] * _silu(gat_ref[sq, :, col(2, h)])
            y_ref[sq, :, col(1, h)] = y_d.astype(BF16)


def rec_mix(qkv3, gat3, gt3, g_c, g_d, c0, n0, m0, s0, nseq, length):
    b, t, _ = qkv3.shape
    tok = lambda w: pl.BlockSpec((nseq, length, w), lambda bi, c: (bi, c, 0))
    st = lambda *shape: pl.BlockSpec((nseq, N_HEADS) + shape, lambda bi, c: (bi, 0, 0, 0))
    vec = pl.BlockSpec((1, BRANCH), lambda bi, c: (0, 0))
    hd = HEAD_DIM
    state_specs = [st(hd, hd), st(1, hd), st(1, GATE_LANES), st(hd, hd)]
    return pl.pallas_call(
        functools.partial(_rec_mix_body, nseq=nseq, length=length),
        grid=(b // nseq, t // length),
        in_specs=[tok(N_OPERAND_BLOCKS * BRANCH), tok(N_GATING_BLOCKS * BRANCH), tok(GATE_LANES), vec, vec] + state_specs,
        out_specs=[tok(2 * BRANCH)] + state_specs,
        out_shape=[jax.ShapeDtypeStruct((b, t, 2 * BRANCH), BF16),
                   jax.ShapeDtypeStruct((b, N_HEADS, hd, hd), F32),
                   jax.ShapeDtypeStruct((b, N_HEADS, 1, hd), F32),
                   jax.ShapeDtypeStruct((b, N_HEADS, 1, GATE_LANES), F32),
                   jax.ShapeDtypeStruct((b, N_HEADS, hd, hd), F32)],
        compiler_params=_ARB2,
        name="rec_mix",
    )(qkv3, gat3, gt3, g_c.reshape(1, -1), g_d.reshape(1, -1), c0, n0, m0, s0)


def _conv_layer(x_tm, sa, sb, g_pre, g_post, w_in, a_w, b_w, b_bias, ln_g, ln_b, w_out, tt, bb, tch):
    t, b, d = x_tm.shape
    x2 = x_tm.reshape(t * b, d)
    proj = norm_matmul(x2, g_pre, w_in, tm=1024, tn=1024)
    y, na, nb = conv_mix(proj.reshape(t, b, -1), sa, sb, a_w, b_w, b_bias, ln_g, ln_b, tt, bb, tch)
    x_out = out_proj(y.reshape(t * b, -1), w_out, g_post, x2, tm=512)
    return x_out.reshape(t, b, d), na, nb


def _rec_layer(x_bm, pos, c0, n0, m0, s0, g_pre, g_post, w_in, w_gate, gate_bias, g_c, g_d, w_out, nseq):
    b, t, d = x_bm.shape
    length = math.gcd(t, CHUNK)
    tm = 1024
    x2 = x_bm.reshape(b * t, d)
    half = HEAD_DIM // 2
    inv = ROPE_BASE ** (-jnp.arange(half, dtype=F32) / half)
    ang = pos[:, None] * inv[None, :]
    reps = max(1, tm // t)
    cos_rows, sin_rows = jnp.tile(jnp.cos(ang), (reps, 1)), jnp.tile(jnp.sin(ang), (reps, 1))
    qkv, gat, gt = rec_in_proj(x2, g_pre, w_in, w_gate, gate_bias, cos_rows, sin_rows, tm)
    m0b = jnp.broadcast_to(m0[:, :, None, None], (b, N_HEADS, 1, GATE_LANES))
    y, c_new, n_new, m_new, s_new = rec_mix(
        qkv.reshape(b, t, -1), gat.reshape(b, t, -1), gt.reshape(b, t, -1), g_c, g_d,
        c0, n0[:, :, None, :], m0b, s0, nseq, length)
    x_out = out_proj(y.reshape(b * t, -1), w_out, g_post, x2, tm=512)
    return x_out.reshape(b, t, d), c_new, n_new[:, :, 0, :], m_new[:, :, 0, 0], s_new


def kernel(x_prompt, x_sample, state_a_conv, state_b_conv, state_c_C, state_c_n, state_c_m, state_d_S, norm_pre, norm_post, w_in_conv, a_conv_w, b_conv_w, b_conv_b, b_ln_g, b_ln_b, w_out_conv, w_in_rec, c_i_b, c_f_b, c_hn_g, d_hn_g, w_out_rec):
    bp, tp, _ = x_prompt.shape
    bs, ts, _ = x_sample.shape
    depth = norm_pre.shape[0]
    pos_p = jnp.arange(tp, dtype=F32)
    pos_s = PAST_LEN + jnp.arange(ts, dtype=F32)
    tm_axes = (1, 0, 2)

    outs = [[] for _ in range(12)]
    xp, xs = x_prompt, x_sample
    for layer in range(depth):
        p = layer // 2
        if layer % 2 == 0:
            wts = (norm_pre[layer], norm_post[layer], w_in_conv[p].astype(BF16), a_conv_w[p], b_conv_w[p],
                   b_conv_b[p], b_ln_g[p], b_ln_b[p], w_out_conv[p].astype(BF16))
            zero_a = jnp.zeros((A_CONV - 1, bp, BRANCH), F32)
            zero_b = jnp.zeros((B_CONV - 1, bp, BRANCH), F32)
            op, na_p, nb_p = _conv_layer(xp.transpose(tm_axes), zero_a, zero_b, *wts, tt=32, bb=8, tch=16)
            os_, na_s, nb_s = _conv_layer(xs.transpose(tm_axes), state_a_conv[p].transpose(tm_axes),
                                          state_b_conv[p].transpose(tm_axes), *wts, tt=ts, bb=16, tch=8)
            xp, xs = op.transpose(tm_axes), os_.transpose(tm_axes)
            new = (na_p.transpose(tm_axes), na_s.transpose(tm_axes), nb_p.transpose(tm_axes), nb_s.transpose(tm_axes))
            for lst, val in zip(outs[0:4], new):
                lst.append(val)
        else:
            w_full = w_in_rec[p]
            w_main = jnp.concatenate([w_full[:, i * BRANCH:(i + 1) * BRANCH] for i in REC_COL_ORDER], axis=1)
            n_main = len(REC_COL_ORDER) * BRANCH
            w_gate = jnp.pad(w_full[:, n_main:], ((0, 0), (0, GATE_LANES - 2 * N_HEADS)))
            gate_bias = jnp.pad(jnp.concatenate([c_i_b[p], c_f_b[p]]), (0, GATE_LANES - 2 * N_HEADS)).reshape(1, GATE_LANES)
            wts = (norm_pre[layer], norm_post[layer], w_main.astype(BF16), w_gate, gate_bias,
                   c_hn_g[p], d_hn_g[p], w_out_rec[p].astype(BF16))
            zc = jnp.zeros((bp, N_HEADS, HEAD_DIM, HEAD_DIM), F32)
            zn = jnp.zeros((bp, N_HEADS, HEAD_DIM), F32)
            zm = jnp.zeros((bp, N_HEADS), F32)
            xp, c1, n1, m1, s1 = _rec_layer(xp, pos_p, zc, zn, zm, zc, *wts, nseq=1)
            xs, c2, n2, m2, s2 = _rec_layer(xs, pos_s, state_c_C[p], state_c_n[p], state_c_m[p], state_d_S[p],
                                            *wts, nseq=4)
            for lst, val in zip(outs[4:12], (c1, c2, n1, n2, m1, m2, s1, s2)):
                lst.append(val)

    return (xp, xs) + tuple(jnp.stack(lst) for lst in outs)
```

```python
import functools
import math

import jax
import jax.numpy as jnp
from jax import lax
from jax.experimental import pallas as pl
from jax.experimental.pallas import tpu as pltpu

F32 = jnp.float32
BF16 = jnp.bfloat16

D_MODEL = 1024
BRANCH = 1024
A_CONV = 3
B_CONV = 31
N_HEADS = 4
HEAD_DIM = 256
CHUNK = 128
ROPE_BASE = 10000.0
PAST_LEN = 16384
EPS = 1e-6
GATE_LANES = 128
VMEM_LIMIT = 48 * 1024 * 1024

_ARB2 = pltpu.CompilerParams(dimension_semantics=("arbitrary", "arbitrary"), vmem_limit_bytes=VMEM_LIMIT)
_ARB1 = pltpu.CompilerParams(dimension_semantics=("arbitrary",), vmem_limit_bytes=VMEM_LIMIT)


def _silu(x):
    return x * jax.nn.sigmoid(x)


def _rms_scale(x):
    return lax.rsqrt(jnp.mean(x * x, axis=-1, keepdims=True) + EPS)


def _norm_matmul_body(x_ref, g_ref, w_ref, o_ref, h_scr):
    @pl.when(pl.program_id(1) == 0)
    def _():
        x = x_ref[...]
        h_scr[...] = (x * _rms_scale(x) * g_ref[...]).astype(BF16)

    o_ref[...] = jnp.dot(h_scr[...], w_ref[...], preferred_element_type=F32)


def norm_matmul(x2, g, w_bf16, tm, tn):
    n, d = x2.shape
    ncols = w_bf16.shape[1]
    return pl.pallas_call(
        _norm_matmul_body,
        grid=(n // tm, ncols // tn),
        in_specs=[pl.BlockSpec((tm, d), lambda i, j: (i, 0)),
                  pl.BlockSpec((1, d), lambda i, j: (0, 0)),
                  pl.BlockSpec((d, tn), lambda i, j: (0, j))],
        out_specs=pl.BlockSpec((tm, tn), lambda i, j: (i, j)),
        out_shape=jax.ShapeDtypeStruct((n, ncols), F32),
        scratch_shapes=[pltpu.VMEM((tm, d), BF16)],
        compiler_params=_ARB2,
        name="norm_matmul",
    )(x2, g.reshape(1, d), w_bf16)


SLAB_VREGS = 16


def _dwconv_block(xp, w_ref, cs, t0, tch, width, one):
    acc = None
    for j in range(width):
        w = jnp.broadcast_to(w_ref[j:j + 1, cs], one.shape) * one
        term = xp[t0 + j:t0 + j + tch, :, cs] * w
        acc = term if acc is None else acc + term
    return acc


def _conv_mix_body(p_ref, sa_ref, sb_ref, aw_ref, bw_ref, bb_ref, lg_ref, lb_ref,
                   y_ref, na_ref, nb_ref, xpa, xpb, cva, cvb, *, tt, tch):
    ti = pl.program_id(1)
    ha, hb = A_CONV - 1, B_CONV - 1

    @pl.when(ti == 0)
    def _():
        xpa[0:ha] = sa_ref[...]
        xpb[0:hb] = sb_ref[...]

    bb = xpa.shape[1]
    ech = max(1, SLAB_VREGS * 8 * 128 // (bb * BRANCH))

    for t0 in range(0, tt, ech):
        ts = slice(t0, t0 + ech)
        xpa[ha + t0:ha + t0 + ech] = p_ref[ts, :, 1 * BRANCH:2 * BRANCH] * p_ref[ts, :, 2 * BRANCH:3 * BRANCH]
        xpb[hb + t0:hb + t0 + ech] = (p_ref[ts, :, 4 * BRANCH:5 * BRANCH]
                                      * jax.nn.sigmoid(p_ref[ts, :, 5 * BRANCH:6 * BRANCH]))

    one = jnp.full((bb, 128), (ti >= 0).astype(F32))
    for c in range(BRANCH // 128):
        cs = slice(c * 128, (c + 1) * 128)
        for t0 in range(0, tt, tch):
            cva[t0:t0 + tch, :, cs] = _dwconv_block(xpa, aw_ref, cs, t0, tch, A_CONV, one)
            cvb[t0:t0 + tch, :, cs] = _dwconv_block(xpb, bw_ref, cs, t0, tch, B_CONV, one)

    for t0 in range(0, tt, ech):
        ts = slice(t0, t0 + ech)
        y_ref[ts, :, 0:BRANCH] = p_ref[ts, :, 0:BRANCH] * cva[ts] * _silu(p_ref[ts, :, 3 * BRANCH:4 * BRANCH])
        zb = cvb[ts] + bb_ref[...]
        mu = jnp.mean(zb, axis=-1, keepdims=True)
        zc = zb - mu
        var = jnp.mean(zc * zc, axis=-1, keepdims=True)
        ln = zc * lax.rsqrt(var + EPS) * lg_ref[...] + lb_ref[...]
        y_ref[ts, :, BRANCH:2 * BRANCH] = _silu(ln) * _silu(p_ref[ts, :, 6 * BRANCH:7 * BRANCH])

    @pl.when(ti == pl.num_programs(1) - 1)
    def _():
        na_ref[...] = xpa[tt:tt + ha]
        nb_ref[...] = xpb[tt:tt + hb]

    xpa[0:ha] = xpa[tt:tt + ha]
    xpb[0:hb] = xpb[tt:tt + hb]


def conv_mix(proj3, sa, sb, a_w, b_w, b_bias, ln_g, ln_b, tt, bb, tch):
    t, b, pc = proj3.shape
    ha, hb = A_CONV - 1, B_CONV - 1
    row = lambda v: v.reshape(1, BRANCH)
    const2 = lambda shape: pl.BlockSpec(shape, lambda bi, ti: (0, 0))
    return pl.pallas_call(
        functools.partial(_conv_mix_body, tt=tt, tch=tch),
        grid=(b // bb, t // tt),
        in_specs=[pl.BlockSpec((tt, bb, pc), lambda bi, ti: (ti, bi, 0)),
                  pl.BlockSpec((ha, bb, BRANCH), lambda bi, ti: (0, bi, 0)),
                  pl.BlockSpec((hb, bb, BRANCH), lambda bi, ti: (0, bi, 0)),
                  const2((A_CONV, BRANCH)), const2((B_CONV, BRANCH)),
                  const2((1, BRANCH)), const2((1, BRANCH)), const2((1, BRANCH))],
        out_specs=[pl.BlockSpec((tt, bb, 2 * BRANCH), lambda bi, ti: (ti, bi, 0)),
                   pl.BlockSpec((ha, bb, BRANCH), lambda bi, ti: (0, bi, 0)),
                   pl.BlockSpec((hb, bb, BRANCH), lambda bi, ti: (0, bi, 0))],
        out_shape=[jax.ShapeDtypeStruct((t, b, 2 * BRANCH), F32),
                   jax.ShapeDtypeStruct((ha, b, BRANCH), F32),
                   jax.ShapeDtypeStruct((hb, b, BRANCH), F32)],
        scratch_shapes=[pltpu.VMEM((ha + tt, bb, BRANCH), F32),
                        pltpu.VMEM((hb + tt, bb, BRANCH), F32),
                        pltpu.VMEM((tt, bb, BRANCH), F32),
                        pltpu.VMEM((tt, bb, BRANCH), F32)],
        compiler_params=_ARB2,
        name="conv_mix",
    )(proj3, sa, sb, a_w, b_w, row(b_bias), row(ln_g), row(ln_b))


def _out_proj_body(y_ref, w_ref, g_ref, x_ref, o_ref):
    z = jnp.dot(y_ref[...].astype(BF16), w_ref[...], preferred_element_type=F32)
    o_ref[...] = x_ref[...] + z * _rms_scale(z) * g_ref[...]


def out_proj(y2, w_bf16, g, x2, tm):
    n, k = y2.shape
    d = w_bf16.shape[1]
    return pl.pallas_call(
        _out_proj_body,
        grid=(n // tm,),
        in_specs=[pl.BlockSpec((tm, k), lambda i: (i, 0)),
                  pl.BlockSpec((k, d), lambda i: (0, 0)),
                  pl.BlockSpec((1, d), lambda i: (0, 0)),
                  pl.BlockSpec((tm, d), lambda i: (i, 0))],
        out_specs=pl.BlockSpec((tm, d), lambda i: (i, 0)),
        out_shape=jax.ShapeDtypeStruct((n, d), F32),
        compiler_params=_ARB1,
        name="out_proj",
    )(y2, w_bf16, g.reshape(1, d), x2)


REC_COL_ORDER = (0, 1, 2, 5, 6, 7, 3, 4, 8)
N_OPERAND_BLOCKS = 6
N_GATING_BLOCKS = 3
ROPE_BLOCKS = (3, 4)


def _rec_in_proj_body(x_ref, g_ref, w_ref, wg_ref, gb_ref, cos_ref, sin_ref, ob_ref, of_ref, gt_ref, h_scr):
    j = pl.program_id(1)

    @pl.when(j == 0)
    def _():
        x = x_ref[...]
        h = x * _rms_scale(x) * g_ref[...]
        h_hi = h.astype(BF16)
        h_scr[...] = h_hi
        h_lo = (h - h_hi.astype(F32)).astype(BF16)
        wg = wg_ref[...]
        w_hi = wg.astype(BF16)
        w_lo = (wg - w_hi.astype(F32)).astype(BF16)
        pre = (jnp.dot(h_hi, w_hi, preferred_element_type=F32) + jnp.dot(h_hi, w_lo, preferred_element_type=F32)
               + jnp.dot(h_lo, w_hi, preferred_element_type=F32) + gb_ref[...])
        log_f = jnp.minimum(pre, 0.0) - jnp.log1p(jnp.exp(-jnp.abs(pre)))
        lane = lax.broadcasted_iota(jnp.int32, pre.shape, 1)
        gt_ref[...] = jnp.where(lane < N_HEADS, pre, log_f)

    z = jnp.dot(h_scr[...], w_ref[...], preferred_element_type=F32)
    is_rope = (j == ROPE_BLOCKS[0]) | (j == ROPE_BLOCKS[1])

    @pl.when((j < N_OPERAND_BLOCKS) & jnp.logical_not(is_rope))
    def _():
        ob_ref[...] = z.astype(BF16)

    @pl.when(is_rope)
    def _():
        cos, sin = cos_ref[...], sin_ref[...]
        half = HEAD_DIM // 2
        for h in range(N_HEADS):
            lo, mid, hi = h * HEAD_DIM, h * HEAD_DIM + half, (h + 1) * HEAD_DIM
            x1, x2 = z[:, lo:mid], z[:, mid:hi]
            ob_ref[:, lo:mid] = (x1 * cos - x2 * sin).astype(BF16)
            ob_ref[:, mid:hi] = (x2 * cos + x1 * sin).astype(BF16)

    @pl.when(j >= N_OPERAND_BLOCKS)
    def _():
        of_ref[...] = z


def rec_in_proj(x2, g, w_bf16, w_gate, gate_bias, cos_rows, sin_rows, tm):
    n, d = x2.shape
    tn = BRANCH
    n_period = cos_rows.shape[0] // tm
    const = lambda shape: pl.BlockSpec(shape, lambda i, j: (0, 0))
    rot = pl.BlockSpec((tm, HEAD_DIM // 2), lambda i, j: (i % n_period, 0))
    return pl.pallas_call(
        _rec_in_proj_body,
        grid=(n // tm, N_OPERAND_BLOCKS + N_GATING_BLOCKS),
        in_specs=[pl.BlockSpec((tm, d), lambda i, j: (i, 0)), const((1, d)),
                  pl.BlockSpec((d, tn), lambda i, j: (0, j)),
                  const((d, GATE_LANES)), const((1, GATE_LANES)), rot, rot],
        out_specs=[pl.BlockSpec((tm, tn), lambda i, j: (i, jnp.minimum(j, N_OPERAND_BLOCKS - 1))),
                   pl.BlockSpec((tm, tn), lambda i, j: (i, jnp.maximum(j - N_OPERAND_BLOCKS, 0))),
                   pl.BlockSpec((tm, GATE_LANES), lambda i, j: (i, 0))],
        out_shape=[jax.ShapeDtypeStruct((n, N_OPERAND_BLOCKS * tn), BF16),
                   jax.ShapeDtypeStruct((n, N_GATING_BLOCKS * tn), F32),
                   jax.ShapeDtypeStruct((n, GATE_LANES), F32)],
        scratch_shapes=[pltpu.VMEM((tm, d), BF16)],
        compiler_params=_ARB2,
        name="rec_in_proj",
    )(x2, g.reshape(1, d), w_bf16, w_gate, gate_bias, cos_rows, sin_rows)


_NT = (((1,), (1,)), ((), ()))
_TN = (((0,), (0,)), ((), ()))
SCALE_K = HEAD_DIM ** -0.5


def _mlstm_chunk(q, k, v, icol, fcol, c_state, n_state, m, r, s):
    eye, tril = r == s, r >= s
    frow = jnp.sum(jnp.where(eye, fcol, 0.0), axis=0, keepdims=True)
    irow = jnp.sum(jnp.where(eye, icol, 0.0), axis=0, keepdims=True)
    b_col = jnp.sum(jnp.where(tril, frow, 0.0), axis=-1, keepdims=True)
    b_row = jnp.sum(jnp.where(r <= s, fcol, 0.0), axis=0, keepdims=True)

    dmat = jnp.where(tril, b_col - b_row + irow, -jnp.inf)
    inter = b_col + m
    m_t = jnp.maximum(jnp.max(dmat, axis=-1, keepdims=True), inter)
    w_intra = jnp.exp(dmat - m_t)
    w_inter = jnp.exp(inter - m_t)

    sc = lax.dot_general(q, k, _NT, preferred_element_type=F32) * (w_intra * SCALE_K)
    num = (jnp.dot(sc.astype(BF16), v, preferred_element_type=F32)
           + w_inter * jnp.dot(q, c_state.astype(BF16), preferred_element_type=F32))
    den = (jnp.sum(sc, axis=-1, keepdims=True)
           + w_inter * jnp.sum(q.astype(F32) * n_state, axis=-1, keepdims=True))
    h = num / jnp.maximum(jnp.abs(den), jnp.exp(-m_t))

    b_last = jnp.sum(frow, axis=-1, keepdims=True)
    dec = b_last - b_col + icol
    m_new = jnp.maximum(b_last + m, jnp.max(dec, axis=0, keepdims=True))
    w_s = jnp.exp(dec - m_new)
    decay = jnp.exp(b_last + m - m_new)
    kw = (k.astype(F32) * SCALE_K) * w_s
    c_new = decay * c_state + lax.dot_general(kw.astype(BF16), v, _TN, preferred_element_type=F32)
    n_new = decay * n_state + jnp.sum(kw, axis=0, keepdims=True)
    return h, c_new, n_new, m_new


def _retention_chunk(q, k, v, s_state, log_g, r, s, length):
    rel = (r - s).astype(F32)
    dmask = jnp.where(rel >= 0, jnp.exp(log_g * jnp.maximum(rel, 0.0)), 0.0)
    idx = lax.broadcasted_iota(jnp.int32, (length, 1), 0).astype(F32)
    q_dec = jnp.exp(log_g * (idx + 1.0))
    k_dec = jnp.exp(log_g * (length - 1.0 - idx))
    g_len = math.exp(log_g * length)
    sc = lax.dot_general(q, k, _NT, preferred_element_type=F32) * (dmask * SCALE_K)
    o = (jnp.dot(sc.astype(BF16), v, preferred_element_type=F32)
         + q_dec * jnp.dot(q, s_state.astype(BF16), preferred_element_type=F32))
    kd = (k.astype(F32) * SCALE_K) * k_dec
    s_new = g_len * s_state + lax.dot_general(kd.astype(BF16), v, _TN, preferred_element_type=F32)
    return o, s_new


def _norm_rows(x):
    mu = jnp.mean(x, axis=-1, keepdims=True)
    xc = x - mu
    return xc * lax.rsqrt(jnp.mean(xc * xc, axis=-1, keepdims=True) + EPS)


def _rec_mix_body(qkv_ref, gat_ref, gt_ref, gc_ref, gd_ref, c0_ref, n0_ref, m0_ref, s0_ref,
                  y_ref, c_ref, n_ref, m_ref, s_ref, *, nseq, length):
    @pl.when(pl.program_id(1) == 0)
    def _():
        c_ref[...] = c0_ref[...]
        n_ref[...] = n0_ref[...]
        m_ref[...] = m0_ref[...]
        s_ref[...] = s0_ref[...]

    r = lax.broadcasted_iota(jnp.int32, (length, length), 0)
    s = lax.broadcasted_iota(jnp.int32, (length, length), 1)
    lane = lax.broadcasted_iota(jnp.int32, (length, GATE_LANES), 1)
    col = lambda blk, h: slice(blk * BRANCH + h * HEAD_DIM, blk * BRANCH + (h + 1) * HEAD_DIM)
    for sq in range(nseq):
        g = gt_ref[sq]
        for h in range(N_HEADS):
            icol = jnp.sum(jnp.where(lane == h, g, 0.0), axis=-1, keepdims=True)
            fcol = jnp.sum(jnp.where(lane == h + N_HEADS, g, 0.0), axis=-1, keepdims=True)
            hc, c_new, n_new, m_new = _mlstm_chunk(
                qkv_ref[sq, :, col(0, h)], qkv_ref[sq, :, col(1, h)], qkv_ref[sq, :, col(2, h)],
                icol, fcol, c_ref[sq, h], n_ref[sq, h], m_ref[sq, h][:, 0:1], r, s)
            c_ref[sq, h] = c_new
            n_ref[sq, h] = n_new
            m_ref[sq, h] = jnp.broadcast_to(m_new, (1, GATE_LANES))
            y_c = (_norm_rows(jax.nn.sigmoid(gat_ref[sq, :, col(0, h)]) * hc) * gc_ref[:, col(0, h)]
                   * _silu(gat_ref[sq, :, col(1, h)]))
            y_ref[sq, :, col(0, h)] = y_c.astype(BF16)

            log_g = math.log(1.0 - 2.0 ** (-5.0 - h))
            o, s_new = _retention_chunk(
                qkv_ref[sq, :, col(3, h)], qkv_ref[sq, :, col(4, h)], qkv_ref[sq, :, col(5, h)],
                s_ref[sq, h], log_g, r, s, length)
            s_ref[sq, h] = s_new
            y_d = _norm_rows(o) * gd_ref[:, col(0, h)] * _silu(gat_ref[sq, :, col(2, h)])
            y_ref[sq, :, col(1, h)] = y_d.astype(BF16)


def rec_mix(qkv3, gat3, gt3, g_c, g_d, c0, n0, m0, s0, nseq, length):
    b, t, _ = qkv3.shape
    tok = lambda w: pl.BlockSpec((nseq, length, w), lambda bi, c: (bi, c, 0))
    st = lambda *shape: pl.BlockSpec((nseq, N_HEADS) + shape, lambda bi, c: (bi, 0, 0, 0))
    vec = pl.BlockSpec((1, BRANCH), lambda bi, c: (0, 0))
    hd = HEAD_DIM
    state_specs = [st(hd, hd), st(1, hd), st(1, GATE_LANES), st(hd, hd)]
    return pl.pallas_call(
        functools.partial(_rec_mix_body, nseq=nseq, length=length),
        grid=(b // nseq, t // length),
        in_specs=[tok(N_OPERAND_BLOCKS * BRANCH), tok(N_GATING_BLOCKS * BRANCH), tok(GATE_LANES), vec, vec] + state_specs,
        out_specs=[tok(2 * BRANCH)] + state_specs,
        out_shape=[jax.ShapeDtypeStruct((b, t, 2 * BRANCH), BF16),
                   jax.ShapeDtypeStruct((b, N_HEADS, hd, hd), F32),
                   jax.ShapeDtypeStruct((b, N_HEADS, 1, hd), F32),
                   jax.ShapeDtypeStruct((b, N_HEADS, 1, GATE_LANES), F32),
                   jax.ShapeDtypeStruct((b, N_HEADS, hd, hd), F32)],
        compiler_params=_ARB2,
        name="rec_mix",
    )(qkv3, gat3, gt3, g_c.reshape(1, -1), g_d.reshape(1, -1), c0, n0, m0, s0)


def _conv_layer(x_tm, sa, sb, g_pre, g_post, w_in, a_w, b_w, b_bias, ln_g, ln_b, w_out, tt, bb, tch):
    t, b, d = x_tm.shape
    x2 = x_tm.reshape(t * b, d)
    proj = norm_matmul(x2, g_pre, w_in, tm=1024, tn=1024)
    y, na, nb = conv_mix(proj.reshape(t, b, -1), sa, sb, a_w, b_w, b_bias, ln_g, ln_b, tt, bb, tch)
    x_out = out_proj(y.reshape(t * b, -1), w_out, g_post, x2, tm=512)
    return x_out.reshape(t, b, d), na, nb


def _rec_layer(x_bm, pos, c0, n0, m0, s0, g_pre, g_post, w_in, w_gate, gate_bias, g_c, g_d, w_out, nseq):
    b, t, d = x_bm.shape
    length = math.gcd(t, CHUNK)
    tm = 1024
    x2 = x_bm.reshape(b * t, d)
    half = HEAD_DIM // 2
    inv = ROPE_BASE ** (-jnp.arange(half, dtype=F32) / half)
    ang = pos[:, None] * inv[None, :]
    reps = max(1, tm // t)
    cos_rows, sin_rows = jnp.tile(jnp.cos(ang), (reps, 1)), jnp.tile(jnp.sin(ang), (reps, 1))
    qkv, gat, gt = rec_in_proj(x2, g_pre, w_in, w_gate, gate_bias, cos_rows, sin_rows, tm)
    m0b = jnp.broadcast_to(m0[:, :, None, None], (b, N_HEADS, 1, GATE_LANES))
    y, c_new, n_new, m_new, s_new = rec_mix(
        qkv.reshape(b, t, -1), gat.reshape(b, t, -1), gt.reshape(b, t, -1), g_c, g_d,
        c0, n0[:, :, None, :], m0b, s0, nseq, length)
    x_out = out_proj(y.reshape(b * t, -1), w_out, g_post, x2, tm=512)
    return x_out.reshape(b, t, d), c_new, n_new[:, :, 0, :], m_new[:, :, 0, 0], s_new


def kernel(x_prompt, x_sample, state_a_conv, state_b_conv, state_c_C, state_c_n, state_c_m, state_d_S, norm_pre, norm_post, w_in_conv, a_conv_w, b_conv_w, b_conv_b, b_ln_g, b_ln_b, w_out_conv, w_in_rec, c_i_b, c_f_b, c_hn_g, d_hn_g, w_out_rec):
    bp, tp, _ = x_prompt.shape
    bs, ts, _ = x_sample.shape
    depth = norm_pre.shape[0]
    pos_p = jnp.arange(tp, dtype=F32)
    pos_s = PAST_LEN + jnp.arange(ts, dtype=F32)
    tm_axes = (1, 0, 2)

    outs = [[] for _ in range(12)]
    xp, xs = x_prompt, x_sample
    for layer in range(depth):
        p = layer // 2
        if layer % 2 == 0:
            wts = (norm_pre[layer], norm_post[layer], w_in_conv[p].astype(BF16), a_conv_w[p], b_conv_w[p],
                   b_conv_b[p], b_ln_g[p], b_ln_b[p], w_out_conv[p].astype(BF16))
            zero_a = jnp.zeros((A_CONV - 1, bp, BRANCH), F32)
            zero_b = jnp.zeros((B_CONV - 1, bp, BRANCH), F32)
            op, na_p, nb_p = _conv_layer(xp.transpose(tm_axes), zero_a, zero_b, *wts, tt=32, bb=8, tch=16)
            os_, na_s, nb_s = _conv_layer(xs.transpose(tm_axes), state_a_conv[p].transpose(tm_axes),
                                          state_b_conv[p].transpose(tm_axes), *wts, tt=ts, bb=16, tch=8)
            xp, xs = op.transpose(tm_axes), os_.transpose(tm_axes)
            new = (na_p.transpose(tm_axes), na_s.transpose(tm_axes), nb_p.transpose(tm_axes), nb_s.transpose(tm_axes))
            for lst, val in zip(outs[0:4], new):
                lst.append(val)
        else:
            w_full = w_in_rec[p]
            w_main = jnp.concatenate([w_full[:, i * BRANCH:(i + 1) * BRANCH] for i in REC_COL_ORDER], axis=1)
            n_main = len(REC_COL_ORDER) * BRANCH
            w_gate = jnp.pad(w_full[:, n_main:], ((0, 0), (0, GATE_LANES - 2 * N_HEADS)))
            gate_bias = jnp.pad(jnp.concatenate([c_i_b[p], c_f_b[p]]), (0, GATE_LANES - 2 * N_HEADS)).reshape(1, GATE_LANES)
            wts = (norm_pre[layer], norm_post[layer], w_main.astype(BF16), w_gate, gate_bias,
                   c_hn_g[p], d_hn_g[p], w_out_rec[p].astype(BF16))
            zc = jnp.zeros((bp, N_HEADS, HEAD_DIM, HEAD_DIM), F32)
            zn = jnp.zeros((bp, N_HEADS, HEAD_DIM), F32)
            zm = jnp.zeros((bp, N_HEADS), F32)
            xp, c1, n1, m1, s1 = _rec_layer(xp, pos_p, zc, zn, zm, zc, *wts, nseq=1)
            xs, c2, n2, m2, s2 = _rec_layer(xs, pos_s, state_c_C[p], state_c_n[p], state_c_m[p], state_d_S[p],
                                            *wts, nseq=4)
            for lst, val in zip(outs[4:12], (c1, c2, n1, n2, m1, m2, s1, s2)):
                lst.append(val)

    return (xp, xs) + tuple(jnp.stack(lst) for lst in outs)
```

```python
import functools
import math

import jax
import jax.numpy as jnp
from jax import lax
from jax.experimental import pallas as pl
from jax.experimental.pallas import tpu as pltpu

F32 = jnp.float32
BF16 = jnp.bfloat16

D_MODEL = 1024
BRANCH = 1024
A_CONV = 3
B_CONV = 31
N_HEADS = 4
HEAD_DIM = 256
CHUNK = 128
ROPE_BASE = 10000.0
PAST_LEN = 16384
EPS = 1e-6
GATE_LANES = 128
VMEM_LIMIT = 56 * 1024 * 1024
IN_PROJ_ROWS = 2048
REC_IN_PROJ_ROWS = 1024

_ARB2 = pltpu.CompilerParams(dimension_semantics=("arbitrary", "arbitrary"), vmem_limit_bytes=VMEM_LIMIT)
_ARB1 = pltpu.CompilerParams(dimension_semantics=("arbitrary",), vmem_limit_bytes=VMEM_LIMIT)


def _silu(x):
    return x * jax.nn.sigmoid(x)


def _rms_scale(x):
    return lax.rsqrt(jnp.mean(x * x, axis=-1, keepdims=True) + EPS)


def _norm_matmul_body(x_ref, g_ref, w_ref, o_ref, h_scr):
    @pl.when(pl.program_id(1) == 0)
    def _():
        x = x_ref[...]
        h_scr[...] = (x * _rms_scale(x) * g_ref[...]).astype(BF16)

    o_ref[...] = jnp.dot(h_scr[...], w_ref[...], preferred_element_type=F32)


def norm_matmul(x2, g, w_bf16, tm, tn):
    n, d = x2.shape
    ncols = w_bf16.shape[1]
    return pl.pallas_call(
        _norm_matmul_body,
        grid=(n // tm, ncols // tn),
        in_specs=[pl.BlockSpec((tm, d), lambda i, j: (i, 0)),
                  pl.BlockSpec((1, d), lambda i, j: (0, 0)),
                  pl.BlockSpec((d, tn), lambda i, j: (0, j))],
        out_specs=pl.BlockSpec((tm, tn), lambda i, j: (i, j)),
        out_shape=jax.ShapeDtypeStruct((n, ncols), F32),
        scratch_shapes=[pltpu.VMEM((tm, d), BF16)],
        compiler_params=_ARB2,
        name="norm_matmul",
    )(x2, g.reshape(1, d), w_bf16)


SLAB_VREGS = 16


def _dwconv_block(xp, w_ref, cs, t0, tch, width, one):
    acc = None
    for j in range(width):
        w = jnp.broadcast_to(w_ref[j:j + 1, cs], one.shape) * one
        term = xp[t0 + j:t0 + j + tch, :, cs] * w
        acc = term if acc is None else acc + term
    return acc


def _conv_mix_body(p_ref, sa_ref, sb_ref, aw_ref, bw_ref, bb_ref, lg_ref, lb_ref,
                   y_ref, na_ref, nb_ref, xpa, xpb, cva, cvb, *, tt, tch):
    ti = pl.program_id(1)
    ha, hb = A_CONV - 1, B_CONV - 1

    @pl.when(ti == 0)
    def _():
        xpa[0:ha] = sa_ref[...]
        xpb[0:hb] = sb_ref[...]

    bb = xpa.shape[1]
    ech = max(1, SLAB_VREGS * 8 * 128 // (bb * BRANCH))

    for t0 in range(0, tt, ech):
        ts = slice(t0, t0 + ech)
        xpa[ha + t0:ha + t0 + ech] = p_ref[ts, :, 1 * BRANCH:2 * BRANCH] * p_ref[ts, :, 2 * BRANCH:3 * BRANCH]
        xpb[hb + t0:hb + t0 + ech] = (p_ref[ts, :, 4 * BRANCH:5 * BRANCH]
                                      * jax.nn.sigmoid(p_ref[ts, :, 5 * BRANCH:6 * BRANCH]))

    one = jnp.full((bb, 128), (ti >= 0).astype(F32))
    for c in range(BRANCH // 128):
        cs = slice(c * 128, (c + 1) * 128)
        for t0 in range(0, tt, tch):
            cva[t0:t0 + tch, :, cs] = _dwconv_block(xpa, aw_ref, cs, t0, tch, A_CONV, one)
            cvb[t0:t0 + tch, :, cs] = _dwconv_block(xpb, bw_ref, cs, t0, tch, B_CONV, one)

    for t0 in range(0, tt, ech):
        ts = slice(t0, t0 + ech)
        y_ref[ts, :, 0:BRANCH] = p_ref[ts, :, 0:BRANCH] * cva[ts] * _silu(p_ref[ts, :, 3 * BRANCH:4 * BRANCH])
        zb = cvb[ts] + bb_ref[...]
        mu = jnp.mean(zb, axis=-1, keepdims=True)
        zc = zb - mu
        var = jnp.mean(zc * zc, axis=-1, keepdims=True)
        ln = zc * lax.rsqrt(var + EPS) * lg_ref[...] + lb_ref[...]
        y_ref[ts, :, BRANCH:2 * BRANCH] = _silu(ln) * _silu(p_ref[ts, :, 6 * BRANCH:7 * BRANCH])

    @pl.when(ti == pl.num_programs(1) - 1)
    def _():
        na_ref[...] = xpa[tt:tt + ha]
        nb_ref[...] = xpb[tt:tt + hb]

    xpa[0:ha] = xpa[tt:tt + ha]
    xpb[0:hb] = xpb[tt:tt + hb]


def conv_mix(proj3, sa, sb, a_w, b_w, b_bias, ln_g, ln_b, tt, bb, tch):
    t, b, pc = proj3.shape
    ha, hb = A_CONV - 1, B_CONV - 1
    row = lambda v: v.reshape(1, BRANCH)
    const2 = lambda shape: pl.BlockSpec(shape, lambda bi, ti: (0, 0))
    return pl.pallas_call(
        functools.partial(_conv_mix_body, tt=tt, tch=tch),
        grid=(b // bb, t // tt),
        in_specs=[pl.BlockSpec((tt, bb, pc), lambda bi, ti: (ti, bi, 0)),
                  pl.BlockSpec((ha, bb, BRANCH), lambda bi, ti: (0, bi, 0)),
                  pl.BlockSpec((hb, bb, BRANCH), lambda bi, ti: (0, bi, 0)),
                  const2((A_CONV, BRANCH)), const2((B_CONV, BRANCH)),
                  const2((1, BRANCH)), const2((1, BRANCH)), const2((1, BRANCH))],
        out_specs=[pl.BlockSpec((tt, bb, 2 * BRANCH), lambda bi, ti: (ti, bi, 0)),
                   pl.BlockSpec((ha, bb, BRANCH), lambda bi, ti: (0, bi, 0)),
                   pl.BlockSpec((hb, bb, BRANCH), lambda bi, ti: (0, bi, 0))],
        out_shape=[jax.ShapeDtypeStruct((t, b, 2 * BRANCH), F32),
                   jax.ShapeDtypeStruct((ha, b, BRANCH), F32),
                   jax.ShapeDtypeStruct((hb, b, BRANCH), F32)],
        scratch_shapes=[pltpu.VMEM((ha + tt, bb, BRANCH), F32),
                        pltpu.VMEM((hb + tt, bb, BRANCH), F32),
                        pltpu.VMEM((tt, bb, BRANCH), F32),
                        pltpu.VMEM((tt, bb, BRANCH), F32)],
        compiler_params=_ARB2,
        name="conv_mix",
    )(proj3, sa, sb, a_w, b_w, row(b_bias), row(ln_g), row(ln_b))


def _out_proj_body(y_ref, w_ref, g_ref, x_ref, o_ref):
    z = jnp.dot(y_ref[...].astype(BF16), w_ref[...], preferred_element_type=F32)
    o_ref[...] = x_ref[...] + z * _rms_scale(z) * g_ref[...]


def out_proj(y2, w_bf16, g, x2, tm):
    n, k = y2.shape
    d = w_bf16.shape[1]
    return pl.pallas_call(
        _out_proj_body,
        grid=(n // tm,),
        in_specs=[pl.BlockSpec((tm, k), lambda i: (i, 0)),
                  pl.BlockSpec((k, d), lambda i: (0, 0)),
                  pl.BlockSpec((1, d), lambda i: (0, 0)),
                  pl.BlockSpec((tm, d), lambda i: (i, 0))],
        out_specs=pl.BlockSpec((tm, d), lambda i: (i, 0)),
        out_shape=jax.ShapeDtypeStruct((n, d), F32),
        compiler_params=_ARB1,
        name="out_proj",
    )(y2, w_bf16, g.reshape(1, d), x2)


REC_COL_ORDER = (0, 1, 2, 5, 6, 7, 3, 4, 8)
N_OPERAND_BLOCKS = 6
N_GATING_BLOCKS = 3
ROPE_BLOCKS = (3, 4)


def _rec_in_proj_body(x_ref, g_ref, w_ref, wg_ref, gb_ref, cos_ref, sin_ref, ob_ref, of_ref, gt_ref, h_scr):
    j = pl.program_id(1)

    @pl.when(j == 0)
    def _():
        x = x_ref[...]
        h = x * _rms_scale(x) * g_ref[...]
        h_hi = h.astype(BF16)
        h_scr[...] = h_hi
        h_lo = (h - h_hi.astype(F32)).astype(BF16)
        wg = wg_ref[...]
        w_hi = wg.astype(BF16)
        w_lo = (wg - w_hi.astype(F32)).astype(BF16)
        pre = (jnp.dot(h_hi, w_hi, preferred_element_type=F32) + jnp.dot(h_hi, w_lo, preferred_element_type=F32)
               + jnp.dot(h_lo, w_hi, preferred_element_type=F32) + gb_ref[...])
        log_f = jnp.minimum(pre, 0.0) - jnp.log1p(jnp.exp(-jnp.abs(pre)))
        lane = lax.broadcasted_iota(jnp.int32, pre.shape, 1)
        gt_ref[...] = jnp.where(lane < N_HEADS, pre, log_f)

    z = jnp.dot(h_scr[...], w_ref[...], preferred_element_type=F32)
    is_rope = (j == ROPE_BLOCKS[0]) | (j == ROPE_BLOCKS[1])

    @pl.when((j < N_OPERAND_BLOCKS) & jnp.logical_not(is_rope))
    def _():
        ob_ref[...] = z.astype(BF16)

    @pl.when(is_rope)
    def _():
        cos, sin = cos_ref[...], sin_ref[...]
        half = HEAD_DIM // 2
        for h in range(N_HEADS):
            lo, mid, hi = h * HEAD_DIM, h * HEAD_DIM + half, (h + 1) * HEAD_DIM
            x1, x2 = z[:, lo:mid], z[:, mid:hi]
            ob_ref[:, lo:mid] = (x1 * cos - x2 * sin).astype(BF16)
            ob_ref[:, mid:hi] = (x2 * cos + x1 * sin).astype(BF16)

    @pl.when(j >= N_OPERAND_BLOCKS)
    def _():
        of_ref[...] = z


def rec_in_proj(x2, g, w_bf16, w_gate, gate_bias, cos_rows, sin_rows, tm):
    n, d = x2.shape
    tn = BRANCH
    n_period = cos_rows.shape[0] // tm
    const = lambda shape: pl.BlockSpec(shape, lambda i, j: (0, 0))
    rot = pl.BlockSpec((tm, HEAD_DIM // 2), lambda i, j: (i % n_period, 0))
    return pl.pallas_call(
        _rec_in_proj_body,
        grid=(n // tm, N_OPERAND_BLOCKS + N_GATING_BLOCKS),
        in_specs=[pl.BlockSpec((tm, d), lambda i, j: (i, 0)), const((1, d)),
                  pl.BlockSpec((d, tn), lambda i, j: (0, j)),
                  const((d, GATE_LANES)), const((1, GATE_LANES)), rot, rot],
        out_specs=[pl.BlockSpec((tm, tn), lambda i, j: (i, jnp.minimum(j, N_OPERAND_BLOCKS - 1))),
                   pl.BlockSpec((tm, tn), lambda i, j: (i, jnp.maximum(j - N_OPERAND_BLOCKS, 0))),
                   pl.BlockSpec((tm, GATE_LANES), lambda i, j: (i, 0))],
        out_shape=[jax.ShapeDtypeStruct((n, N_OPERAND_BLOCKS * tn), BF16),
                   jax.ShapeDtypeStruct((n, N_GATING_BLOCKS * tn), F32),
                   jax.ShapeDtypeStruct((n, GATE_LANES), F32)],
        scratch_shapes=[pltpu.VMEM((tm, d), BF16)],
        compiler_params=_ARB2,
        name="rec_in_proj",
    )(x2, g.reshape(1, d), w_bf16, w_gate, gate_bias, cos_rows, sin_rows)


_NT = (((1,), (1,)), ((), ()))
_TN = (((0,), (0,)), ((), ()))
SCALE_K = HEAD_DIM ** -0.5


def _mlstm_chunk(q, k, v, icol, fcol, c_state, n_state, m, r, s):
    eye, tril = r == s, r >= s
    frow = jnp.sum(jnp.where(eye, fcol, 0.0), axis=0, keepdims=True)
    irow = jnp.sum(jnp.where(eye, icol, 0.0), axis=0, keepdims=True)
    b_col = jnp.sum(jnp.where(tril, frow, 0.0), axis=-1, keepdims=True)
    b_row = jnp.sum(jnp.where(r <= s, fcol, 0.0), axis=0, keepdims=True)

    dmat = jnp.where(tril, b_col - b_row + irow, -jnp.inf)
    inter = b_col + m
    m_t = jnp.maximum(jnp.max(dmat, axis=-1, keepdims=True), inter)
    w_intra = jnp.exp(dmat - m_t)
    w_inter = jnp.exp(inter - m_t)

    sc = lax.dot_general(q, k, _NT, preferred_element_type=F32) * (w_intra * SCALE_K)
    num = (jnp.dot(sc.astype(BF16), v, preferred_element_type=F32)
           + w_inter * jnp.dot(q, c_state.astype(BF16), preferred_element_type=F32))
    den = (jnp.sum(sc, axis=-1, keepdims=True)
           + w_inter * jnp.sum(q.astype(F32) * n_state, axis=-1, keepdims=True))
    h = num / jnp.maximum(jnp.abs(den), jnp.exp(-m_t))

    b_last = jnp.sum(frow, axis=-1, keepdims=True)
    dec = b_last - b_col + icol
    m_new = jnp.maximum(b_last + m, jnp.max(dec, axis=0, keepdims=True))
    w_s = jnp.exp(dec - m_new)
    decay = jnp.exp(b_last + m - m_new)
    kw = (k.astype(F32) * SCALE_K) * w_s
    c_new = decay * c_state + lax.dot_general(kw.astype(BF16), v, _TN, preferred_element_type=F32)
    n_new = decay * n_state + jnp.sum(kw, axis=0, keepdims=True)
    return h, c_new, n_new, m_new


def _retention_chunk(q, k, v, s_state, log_g, r, s, length):
    rel = (r - s).astype(F32)
    dmask = jnp.where(rel >= 0, jnp.exp(log_g * jnp.maximum(rel, 0.0)), 0.0)
    idx = lax.broadcasted_iota(jnp.int32, (length, 1), 0).astype(F32)
    q_dec = jnp.exp(log_g * (idx + 1.0))
    k_dec = jnp.exp(log_g * (length - 1.0 - idx))
    g_len = math.exp(log_g * length)
    sc = lax.dot_general(q, k, _NT, preferred_element_type=F32) * (dmask * SCALE_K)
    o = (jnp.dot(sc.astype(BF16), v, preferred_element_type=F32)
         + q_dec * jnp.dot(q, s_state.astype(BF16), preferred_element_type=F32))
    kd = (k.astype(F32) * SCALE_K) * k_dec
    s_new = g_len * s_state + lax.dot_general(kd.astype(BF16), v, _TN, preferred_element_type=F32)
    return o, s_new


def _norm_rows(x):
    mu = jnp.mean(x, axis=-1, keepdims=True)
    xc = x - mu
    return xc * lax.rsqrt(jnp.mean(xc * xc, axis=-1, keepdims=True) + EPS)


def _rec_mix_body(qkv_ref, gat_ref, gt_ref, gc_ref, gd_ref, c0_ref, n0_ref, m0_ref, s0_ref,
                  y_ref, c_ref, n_ref, m_ref, s_ref, *, nseq, length):
    @pl.when(pl.program_id(1) == 0)
    def _():
        c_ref[...] = c0_ref[...]
        n_ref[...] = n0_ref[...]
        m_ref[...] = m0_ref[...]
        s_ref[...] = s0_ref[...]

    r = lax.broadcasted_iota(jnp.int32, (length, length), 0)
    s = lax.broadcasted_iota(jnp.int32, (length, length), 1)
    lane = lax.broadcasted_iota(jnp.int32, (length, GATE_LANES), 1)
    col = lambda blk, h: slice(blk * BRANCH + h * HEAD_DIM, blk * BRANCH + (h + 1) * HEAD_DIM)
    for sq in range(nseq):
        g = gt_ref[sq]
        for h in range(N_HEADS):
            icol = jnp.sum(jnp.where(lane == h, g, 0.0), axis=-1, keepdims=True)
            fcol = jnp.sum(jnp.where(lane == h + N_HEADS, g, 0.0), axis=-1, keepdims=True)
            hc, c_new, n_new, m_new = _mlstm_chunk(
                qkv_ref[sq, :, col(0, h)], qkv_ref[sq, :, col(1, h)], qkv_ref[sq, :, col(2, h)],
                icol, fcol, c_ref[sq, h], n_ref[sq, h], m_ref[sq, h][:, 0:1], r, s)
            c_ref[sq, h] = c_new
            n_ref[sq, h] = n_new
            m_ref[sq, h] = jnp.broadcast_to(m_new, (1, GATE_LANES))
            y_c = (_norm_rows(jax.nn.sigmoid(gat_ref[sq, :, col(0, h)]) * hc) * gc_ref[:, col(0, h)]
                   * _silu(gat_ref[sq, :, col(1, h)]))
            y_ref[sq, :, col(0, h)] = y_c.astype(BF16)

            log_g = math.log(1.0 - 2.0 ** (-5.0 - h))
            o, s_new = _retention_chunk(
                qkv_ref[sq, :, col(3, h)], qkv_ref[sq, :, col(4, h)], qkv_ref[sq, :, col(5, h)],
                s_ref[sq, h], log_g, r, s, length)
            s_ref[sq, h] = s_new
            y_d = _norm_rows(o) * gd_ref[:, col(0, h)] * _silu(gat_ref[sq, :, col(2, h)])
            y_ref[sq, :, col(1, h)] = y_d.astype(BF16)


def rec_mix(qkv3, gat3, gt3, g_c, g_d, c0, n0, m0, s0, nseq, length):
    b, t, _ = qkv3.shape
    tok = lambda w: pl.BlockSpec((nseq, length, w), lambda bi, c: (bi, c, 0))
    st = lambda *shape: pl.BlockSpec((nseq, N_HEADS) + shape, lambda bi, c: (bi, 0, 0, 0))
    vec = pl.BlockSpec((1, BRANCH), lambda bi, c: (0, 0))
    hd = HEAD_DIM
    state_specs = [st(hd, hd), st(1, hd), st(1, GATE_LANES), st(hd, hd)]
    return pl.pallas_call(
        functools.partial(_rec_mix_body, nseq=nseq, length=length),
        grid=(b // nseq, t // length),
        in_specs=[tok(N_OPERAND_BLOCKS * BRANCH), tok(N_GATING_BLOCKS * BRANCH), tok(GATE_LANES), vec, vec] + state_specs,
        out_specs=[tok(2 * BRANCH)] + state_specs,
        out_shape=[jax.ShapeDtypeStruct((b, t, 2 * BRANCH), BF16),
                   jax.ShapeDtypeStruct((b, N_HEADS, hd, hd), F32),
                   jax.ShapeDtypeStruct((b, N_HEADS, 1, hd), F32),
                   jax.ShapeDtypeStruct((b, N_HEADS, 1, GATE_LANES), F32),
                   jax.ShapeDtypeStruct((b, N_HEADS, hd, hd), F32)],
        compiler_params=_ARB2,
        name="rec_mix",
    )(qkv3, gat3, gt3, g_c.reshape(1, -1), g_d.reshape(1, -1), c0, n0, m0, s0)


def _conv_layer(x_tm, sa, sb, g_pre, g_post, w_in, a_w, b_w, b_bias, ln_g, ln_b, w_out, tt, bb, tch):
    t, b, d = x_tm.shape
    x2 = x_tm.reshape(t * b, d)
    proj = norm_matmul(x2, g_pre, w_in, tm=min(IN_PROJ_ROWS, t * b), tn=1024)
    y, na, nb = conv_mix(proj.reshape(t, b, -1), sa, sb, a_w, b_w, b_bias, ln_g, ln_b, tt, bb, tch)
    x_out = out_proj(y.reshape(t * b, -1), w_out, g_post, x2, tm=512)
    return x_out.reshape(t, b, d), na, nb


def _rec_layer(x_bm, pos, c0, n0, m0, s0, g_pre, g_post, w_in, w_gate, gate_bias, g_c, g_d, w_out, nseq):
    b, t, d = x_bm.shape
    length = math.gcd(t, CHUNK)
    tm = min(REC_IN_PROJ_ROWS, b * t)
    x2 = x_bm.reshape(b * t, d)
    half = HEAD_DIM // 2
    inv = ROPE_BASE ** (-jnp.arange(half, dtype=F32) / half)
    ang = pos[:, None] * inv[None, :]
    reps = max(1, tm // t)
    cos_rows, sin_rows = jnp.tile(jnp.cos(ang), (reps, 1)), jnp.tile(jnp.sin(ang), (reps, 1))
    qkv, gat, gt = rec_in_proj(x2, g_pre, w_in, w_gate, gate_bias, cos_rows, sin_rows, tm)
    m0b = jnp.broadcast_to(m0[:, :, None, None], (b, N_HEADS, 1, GATE_LANES))
    y, c_new, n_new, m_new, s_new = rec_mix(
        qkv.reshape(b, t, -1), gat.reshape(b, t, -1), gt.reshape(b, t, -1), g_c, g_d,
        c0, n0[:, :, None, :], m0b, s0, nseq, length)
    x_out = out_proj(y.reshape(b * t, -1), w_out, g_post, x2, tm=512)
    return x_out.reshape(b, t, d), c_new, n_new[:, :, 0, :], m_new[:, :, 0, 0], s_new


def kernel(x_prompt, x_sample, state_a_conv, state_b_conv, state_c_C, state_c_n, state_c_m, state_d_S, norm_pre, norm_post, w_in_conv, a_conv_w, b_conv_w, b_conv_b, b_ln_g, b_ln_b, w_out_conv, w_in_rec, c_i_b, c_f_b, c_hn_g, d_hn_g, w_out_rec):
    bp, tp, _ = x_prompt.shape
    bs, ts, _ = x_sample.shape
    depth = norm_pre.shape[0]
    pos_p = jnp.arange(tp, dtype=F32)
    pos_s = PAST_LEN + jnp.arange(ts, dtype=F32)
    tm_axes = (1, 0, 2)

    outs = [[] for _ in range(12)]
    xp, xs = x_prompt, x_sample
    for layer in range(depth):
        p = layer // 2
        if layer % 2 == 0:
            wts = (norm_pre[layer], norm_post[layer], w_in_conv[p].astype(BF16), a_conv_w[p], b_conv_w[p],
                   b_conv_b[p], b_ln_g[p], b_ln_b[p], w_out_conv[p].astype(BF16))
            zero_a = jnp.zeros((A_CONV - 1, bp, BRANCH), F32)
            zero_b = jnp.zeros((B_CONV - 1, bp, BRANCH), F32)
            op, na_p, nb_p = _conv_layer(xp.transpose(tm_axes), zero_a, zero_b, *wts, tt=32, bb=8, tch=16)
            os_, na_s, nb_s = _conv_layer(xs.transpose(tm_axes), state_a_conv[p].transpose(tm_axes),
                                          state_b_conv[p].transpose(tm_axes), *wts, tt=ts, bb=16, tch=8)
            xp, xs = op.transpose(tm_axes), os_.transpose(tm_axes)
            new = (na_p.transpose(tm_axes), na_s.transpose(tm_axes), nb_p.transpose(tm_axes), nb_s.transpose(tm_axes))
            for lst, val in zip(outs[0:4], new):
                lst.append(val)
        else:
            w_full = w_in_rec[p]
            w_main = jnp.concatenate([w_full[:, i * BRANCH:(i + 1) * BRANCH] for i in REC_COL_ORDER], axis=1)
            n_main = len(REC_COL_ORDER) * BRANCH
            w_gate = jnp.pad(w_full[:, n_main:], ((0, 0), (0, GATE_LANES - 2 * N_HEADS)))
            gate_bias = jnp.pad(jnp.concatenate([c_i_b[p], c_f_b[p]]), (0, GATE_LANES - 2 * N_HEADS)).reshape(1, GATE_LANES)
            wts = (norm_pre[layer], norm_post[layer], w_main.astype(BF16), w_gate, gate_bias,
                   c_hn_g[p], d_hn_g[p], w_out_rec[p].astype(BF16))
            zc = jnp.zeros((bp, N_HEADS, HEAD_DIM, HEAD_DIM), F32)
            zn = jnp.zeros((bp, N_HEADS, HEAD_DIM), F32)
            zm = jnp.zeros((bp, N_HEADS), F32)
            xp, c1, n1, m1, s1 = _rec_layer(xp, pos_p, zc, zn, zm, zc, *wts, nseq=1)
            xs, c2, n2, m2, s2 = _rec_layer(xs, pos_s, state_c_C[p], state_c_n[p], state_c_m[p], state_d_S[p],
                                            *wts, nseq=4)
            for lst, val in zip(outs[4:12], (c1, c2, n1, n2, m1, m2, s1, s2)):
                lst.append(val)

    return (xp, xs) + tuple(jnp.stack(lst) for lst in outs)
```

```python
import functools
import math

import jax
import jax.numpy as jnp
from jax import lax
from jax.experimental import pallas as pl
from jax.experimental.pallas import tpu as pltpu

F32 = jnp.float32
BF16 = jnp.bfloat16

D_MODEL = 1024
BRANCH = 1024
A_CONV = 3
B_CONV = 31
N_HEADS = 4
HEAD_DIM = 256
CHUNK = 256
ROPE_BASE = 10000.0
PAST_LEN = 16384
EPS = 1e-6
GATE_LANES = 128
VMEM_LIMIT = 56 * 1024 * 1024
IN_PROJ_ROWS = 2048
REC_IN_PROJ_ROWS = 1024

_ARB2 = pltpu.CompilerParams(dimension_semantics=("arbitrary", "arbitrary"), vmem_limit_bytes=VMEM_LIMIT)
_ARB1 = pltpu.CompilerParams(dimension_semantics=("arbitrary",), vmem_limit_bytes=VMEM_LIMIT)


def _silu(x):
    return x * jax.nn.sigmoid(x)


def _rms_scale(x):
    return lax.rsqrt(jnp.mean(x * x, axis=-1, keepdims=True) + EPS)


def _norm_matmul_body(x_ref, g_ref, w_ref, o_ref, h_scr):
    @pl.when(pl.program_id(1) == 0)
    def _():
        x = x_ref[...]
        h_scr[...] = (x * _rms_scale(x) * g_ref[...]).astype(BF16)

    o_ref[...] = jnp.dot(h_scr[...], w_ref[...], preferred_element_type=F32)


def norm_matmul(x2, g, w_bf16, tm, tn):
    n, d = x2.shape
    ncols = w_bf16.shape[1]
    return pl.pallas_call(
        _norm_matmul_body,
        grid=(n // tm, ncols // tn),
        in_specs=[pl.BlockSpec((tm, d), lambda i, j: (i, 0)),
                  pl.BlockSpec((1, d), lambda i, j: (0, 0)),
                  pl.BlockSpec((d, tn), lambda i, j: (0, j))],
        out_specs=pl.BlockSpec((tm, tn), lambda i, j: (i, j)),
        out_shape=jax.ShapeDtypeStruct((n, ncols), F32),
        scratch_shapes=[pltpu.VMEM((tm, d), BF16)],
        compiler_params=_ARB2,
        name="norm_matmul",
    )(x2, g.reshape(1, d), w_bf16)


SLAB_VREGS = 16


def _dwconv_block(xp, w_ref, cs, t0, tch, width, one):
    acc = None
    for j in range(width):
        w = jnp.broadcast_to(w_ref[j:j + 1, cs], one.shape) * one
        term = xp[t0 + j:t0 + j + tch, :, cs] * w
        acc = term if acc is None else acc + term
    return acc


def _conv_mix_body(p_ref, sa_ref, sb_ref, aw_ref, bw_ref, bb_ref, lg_ref, lb_ref,
                   y_ref, na_ref, nb_ref, xpa, xpb, cva, cvb, *, tt, tch):
    ti = pl.program_id(1)
    ha, hb = A_CONV - 1, B_CONV - 1

    @pl.when(ti == 0)
    def _():
        xpa[0:ha] = sa_ref[...]
        xpb[0:hb] = sb_ref[...]

    bb = xpa.shape[1]
    ech = max(1, SLAB_VREGS * 8 * 128 // (bb * BRANCH))

    for t0 in range(0, tt, ech):
        ts = slice(t0, t0 + ech)
        xpa[ha + t0:ha + t0 + ech] = p_ref[ts, :, 1 * BRANCH:2 * BRANCH] * p_ref[ts, :, 2 * BRANCH:3 * BRANCH]
        xpb[hb + t0:hb + t0 + ech] = (p_ref[ts, :, 4 * BRANCH:5 * BRANCH]
                                      * jax.nn.sigmoid(p_ref[ts, :, 5 * BRANCH:6 * BRANCH]))

    one = jnp.full((bb, 128), (ti >= 0).astype(F32))
    for c in range(BRANCH // 128):
        cs = slice(c * 128, (c + 1) * 128)
        for t0 in range(0, tt, tch):
            cva[t0:t0 + tch, :, cs] = _dwconv_block(xpa, aw_ref, cs, t0, tch, A_CONV, one)
            cvb[t0:t0 + tch, :, cs] = _dwconv_block(xpb, bw_ref, cs, t0, tch, B_CONV, one)

    for t0 in range(0, tt, ech):
        ts = slice(t0, t0 + ech)
        y_ref[ts, :, 0:BRANCH] = p_ref[ts, :, 0:BRANCH] * cva[ts] * _silu(p_ref[ts, :, 3 * BRANCH:4 * BRANCH])
        zb = cvb[ts] + bb_ref[...]
        mu = jnp.mean(zb, axis=-1, keepdims=True)
        zc = zb - mu
        var = jnp.mean(zc * zc, axis=-1, keepdims=True)
        ln = zc * lax.rsqrt(var + EPS) * lg_ref[...] + lb_ref[...]
        y_ref[ts, :, BRANCH:2 * BRANCH] = _silu(ln) * _silu(p_ref[ts, :, 6 * BRANCH:7 * BRANCH])

    @pl.when(ti == pl.num_programs(1) - 1)
    def _():
        na_ref[...] = xpa[tt:tt + ha]
        nb_ref[...] = xpb[tt:tt + hb]

    xpa[0:ha] = xpa[tt:tt + ha]
    xpb[0:hb] = xpb[tt:tt + hb]


def conv_mix(proj3, sa, sb, a_w, b_w, b_bias, ln_g, ln_b, tt, bb, tch):
    t, b, pc = proj3.shape
    ha, hb = A_CONV - 1, B_CONV - 1
    row = lambda v: v.reshape(1, BRANCH)
    const2 = lambda shape: pl.BlockSpec(shape, lambda bi, ti: (0, 0))
    return pl.pallas_call(
        functools.partial(_conv_mix_body, tt=tt, tch=tch),
        grid=(b // bb, t // tt),
        in_specs=[pl.BlockSpec((tt, bb, pc), lambda bi, ti: (ti, bi, 0)),
                  pl.BlockSpec((ha, bb, BRANCH), lambda bi, ti: (0, bi, 0)),
                  pl.BlockSpec((hb, bb, BRANCH), lambda bi, ti: (0, bi, 0)),
                  const2((A_CONV, BRANCH)), const2((B_CONV, BRANCH)),
                  const2((1, BRANCH)), const2((1, BRANCH)), const2((1, BRANCH))],
        out_specs=[pl.BlockSpec((tt, bb, 2 * BRANCH), lambda bi, ti: (ti, bi, 0)),
                   pl.BlockSpec((ha, bb, BRANCH), lambda bi, ti: (0, bi, 0)),
                   pl.BlockSpec((hb, bb, BRANCH), lambda bi, ti: (0, bi, 0))],
        out_shape=[jax.ShapeDtypeStruct((t, b, 2 * BRANCH), F32),
                   jax.ShapeDtypeStruct((ha, b, BRANCH), F32),
                   jax.ShapeDtypeStruct((hb, b, BRANCH), F32)],
        scratch_shapes=[pltpu.VMEM((ha + tt, bb, BRANCH), F32),
                        pltpu.VMEM((hb + tt, bb, BRANCH), F32),
                        pltpu.VMEM((tt, bb, BRANCH), F32),
                        pltpu.VMEM((tt, bb, BRANCH), F32)],
        compiler_params=_ARB2,
        name="conv_mix",
    )(proj3, sa, sb, a_w, b_w, row(b_bias), row(ln_g), row(ln_b))


def _out_proj_body(y_ref, w_ref, g_ref, x_ref, o_ref):
    z = jnp.dot(y_ref[...].astype(BF16), w_ref[...], preferred_element_type=F32)
    o_ref[...] = x_ref[...] + z * _rms_scale(z) * g_ref[...]


def out_proj(y2, w_bf16, g, x2, tm):
    n, k = y2.shape
    d = w_bf16.shape[1]
    return pl.pallas_call(
        _out_proj_body,
        grid=(n // tm,),
        in_specs=[pl.BlockSpec((tm, k), lambda i: (i, 0)),
                  pl.BlockSpec((k, d), lambda i: (0, 0)),
                  pl.BlockSpec((1, d), lambda i: (0, 0)),
                  pl.BlockSpec((tm, d), lambda i: (i, 0))],
        out_specs=pl.BlockSpec((tm, d), lambda i: (i, 0)),
        out_shape=jax.ShapeDtypeStruct((n, d), F32),
        compiler_params=_ARB1,
        name="out_proj",
    )(y2, w_bf16, g.reshape(1, d), x2)


REC_COL_ORDER = (0, 1, 2, 5, 6, 7, 3, 4, 8)
N_OPERAND_BLOCKS = 6
N_GATING_BLOCKS = 3
ROPE_BLOCKS = (3, 4)


def _rec_in_proj_body(x_ref, g_ref, w_ref, wg_ref, gb_ref, cos_ref, sin_ref, ob_ref, of_ref, gt_ref, h_scr):
    j = pl.program_id(1)

    @pl.when(j == 0)
    def _():
        x = x_ref[...]
        h = x * _rms_scale(x) * g_ref[...]
        h_hi = h.astype(BF16)
        h_scr[...] = h_hi
        h_lo = (h - h_hi.astype(F32)).astype(BF16)
        wg = wg_ref[...]
        w_hi = wg.astype(BF16)
        w_lo = (wg - w_hi.astype(F32)).astype(BF16)
        pre = (jnp.dot(h_hi, w_hi, preferred_element_type=F32) + jnp.dot(h_hi, w_lo, preferred_element_type=F32)
               + jnp.dot(h_lo, w_hi, preferred_element_type=F32) + gb_ref[...])
        log_f = jnp.minimum(pre, 0.0) - jnp.log1p(jnp.exp(-jnp.abs(pre)))
        lane = lax.broadcasted_iota(jnp.int32, pre.shape, 1)
        gt_ref[...] = jnp.where(lane < N_HEADS, pre, log_f)

    z = jnp.dot(h_scr[...], w_ref[...], preferred_element_type=F32)
    is_rope = (j == ROPE_BLOCKS[0]) | (j == ROPE_BLOCKS[1])

    @pl.when((j < N_OPERAND_BLOCKS) & jnp.logical_not(is_rope))
    def _():
        ob_ref[...] = z.astype(BF16)

    @pl.when(is_rope)
    def _():
        cos, sin = cos_ref[...], sin_ref[...]
        half = HEAD_DIM // 2
        for h in range(N_HEADS):
            lo, mid, hi = h * HEAD_DIM, h * HEAD_DIM + half, (h + 1) * HEAD_DIM
            x1, x2 = z[:, lo:mid], z[:, mid:hi]
            ob_ref[:, lo:mid] = (x1 * cos - x2 * sin).astype(BF16)
            ob_ref[:, mid:hi] = (x2 * cos + x1 * sin).astype(BF16)

    @pl.when(j >= N_OPERAND_BLOCKS)
    def _():
        of_ref[...] = z


def rec_in_proj(x2, g, w_bf16, w_gate, gate_bias, cos_rows, sin_rows, tm):
    n, d = x2.shape
    tn = BRANCH
    n_period = cos_rows.shape[0] // tm
    const = lambda shape: pl.BlockSpec(shape, lambda i, j: (0, 0))
    rot = pl.BlockSpec((tm, HEAD_DIM // 2), lambda i, j: (i % n_period, 0))
    return pl.pallas_call(
        _rec_in_proj_body,
        grid=(n // tm, N_OPERAND_BLOCKS + N_GATING_BLOCKS),
        in_specs=[pl.BlockSpec((tm, d), lambda i, j: (i, 0)), const((1, d)),
                  pl.BlockSpec((d, tn), lambda i, j: (0, j)),
                  const((d, GATE_LANES)), const((1, GATE_LANES)), rot, rot],
        out_specs=[pl.BlockSpec((tm, tn), lambda i, j: (i, jnp.minimum(j, N_OPERAND_BLOCKS - 1))),
                   pl.BlockSpec((tm, tn), lambda i, j: (i, jnp.maximum(j - N_OPERAND_BLOCKS, 0))),
                   pl.BlockSpec((tm, GATE_LANES), lambda i, j: (i, 0))],
        out_shape=[jax.ShapeDtypeStruct((n, N_OPERAND_BLOCKS * tn), BF16),
                   jax.ShapeDtypeStruct((n, N_GATING_BLOCKS * tn), F32),
                   jax.ShapeDtypeStruct((n, GATE_LANES), F32)],
        scratch_shapes=[pltpu.VMEM((tm, d), BF16)],
        compiler_params=_ARB2,
        name="rec_in_proj",
    )(x2, g.reshape(1, d), w_bf16, w_gate, gate_bias, cos_rows, sin_rows)


_NT = (((1,), (1,)), ((), ()))
_TN = (((0,), (0,)), ((), ()))
SCALE_K = HEAD_DIM ** -0.5


def _mlstm_chunk(q, k, v, icol, fcol, c_state, n_state, m, r, s):
    eye, tril = r == s, r >= s
    frow = jnp.sum(jnp.where(eye, fcol, 0.0), axis=0, keepdims=True)
    irow = jnp.sum(jnp.where(eye, icol, 0.0), axis=0, keepdims=True)
    b_col = jnp.sum(jnp.where(tril, frow, 0.0), axis=-1, keepdims=True)
    b_row = jnp.sum(jnp.where(r <= s, fcol, 0.0), axis=0, keepdims=True)

    dmat = jnp.where(tril, b_col - b_row + irow, -jnp.inf)
    inter = b_col + m
    m_t = jnp.maximum(jnp.max(dmat, axis=-1, keepdims=True), inter)
    w_intra = jnp.exp(dmat - m_t)
    w_inter = jnp.exp(inter - m_t)

    sc = lax.dot_general(q, k, _NT, preferred_element_type=F32) * (w_intra * SCALE_K)
    num = (jnp.dot(sc.astype(BF16), v, preferred_element_type=F32)
           + w_inter * jnp.dot(q, c_state.astype(BF16), preferred_element_type=F32))
    den = (jnp.sum(sc, axis=-1, keepdims=True)
           + w_inter * jnp.sum(q.astype(F32) * n_state, axis=-1, keepdims=True))
    h = num / jnp.maximum(jnp.abs(den), jnp.exp(-m_t))

    b_last = jnp.sum(frow, axis=-1, keepdims=True)
    dec = b_last - b_col + icol
    m_new = jnp.maximum(b_last + m, jnp.max(dec, axis=0, keepdims=True))
    w_s = jnp.exp(dec - m_new)
    decay = jnp.exp(b_last + m - m_new)
    kw = (k.astype(F32) * SCALE_K) * w_s
    c_new = decay * c_state + lax.dot_general(kw.astype(BF16), v, _TN, preferred_element_type=F32)
    n_new = decay * n_state + jnp.sum(kw, axis=0, keepdims=True)
    return h, c_new, n_new, m_new


def _retention_chunk(q, k, v, s_state, log_g, r, s, length):
    rel = (r - s).astype(F32)
    dmask = jnp.where(rel >= 0, jnp.exp(log_g * jnp.maximum(rel, 0.0)), 0.0)
    idx = lax.broadcasted_iota(jnp.int32, (length, 1), 0).astype(F32)
    q_dec = jnp.exp(log_g * (idx + 1.0))
    k_dec = jnp.exp(log_g * (length - 1.0 - idx))
    g_len = math.exp(log_g * length)
    sc = lax.dot_general(q, k, _NT, preferred_element_type=F32) * (dmask * SCALE_K)
    o = (jnp.dot(sc.astype(BF16), v, preferred_element_type=F32)
         + q_dec * jnp.dot(q, s_state.astype(BF16), preferred_element_type=F32))
    kd = (k.astype(F32) * SCALE_K) * k_dec
    s_new = g_len * s_state + lax.dot_general(kd.astype(BF16), v, _TN, preferred_element_type=F32)
    return o, s_new


def _norm_rows(x):
    mu = jnp.mean(x, axis=-1, keepdims=True)
    xc = x - mu
    return xc * lax.rsqrt(jnp.mean(xc * xc, axis=-1, keepdims=True) + EPS)


def _rec_mix_body(qkv_ref, gat_ref, gt_ref, gc_ref, gd_ref, c0_ref, n0_ref, m0_ref, s0_ref,
                  y_ref, c_ref, n_ref, m_ref, s_ref, *, nseq, length):
    @pl.when(pl.program_id(1) == 0)
    def _():
        c_ref[...] = c0_ref[...]
        n_ref[...] = n0_ref[...]
        m_ref[...] = m0_ref[...]
        s_ref[...] = s0_ref[...]

    r = lax.broadcasted_iota(jnp.int32, (length, length), 0)
    s = lax.broadcasted_iota(jnp.int32, (length, length), 1)
    lane = lax.broadcasted_iota(jnp.int32, (length, GATE_LANES), 1)
    col = lambda blk, h: slice(blk * BRANCH + h * HEAD_DIM, blk * BRANCH + (h + 1) * HEAD_DIM)
    for sq in range(nseq):
        g = gt_ref[sq]
        for h in range(N_HEADS):
            icol = jnp.sum(jnp.where(lane == h, g, 0.0), axis=-1, keepdims=True)
            fcol = jnp.sum(jnp.where(lane == h + N_HEADS, g, 0.0), axis=-1, keepdims=True)
            hc, c_new, n_new, m_new = _mlstm_chunk(
                qkv_ref[sq, :, col(0, h)], qkv_ref[sq, :, col(1, h)], qkv_ref[sq, :, col(2, h)],
                icol, fcol, c_ref[sq, h], n_ref[sq, h], m_ref[sq, h][:, 0:1], r, s)
            c_ref[sq, h] = c_new
            n_ref[sq, h] = n_new
            m_ref[sq, h] = jnp.broadcast_to(m_new, (1, GATE_LANES))
            y_c = (_norm_rows(jax.nn.sigmoid(gat_ref[sq, :, col(0, h)]) * hc) * gc_ref[:, col(0, h)]
                   * _silu(gat_ref[sq, :, col(1, h)]))
            y_ref[sq, :, col(0, h)] = y_c.astype(BF16)

            log_g = math.log(1.0 - 2.0 ** (-5.0 - h))
            o, s_new = _retention_chunk(
                qkv_ref[sq, :, col(3, h)], qkv_ref[sq, :, col(4, h)], qkv_ref[sq, :, col(5, h)],
                s_ref[sq, h], log_g, r, s, length)
            s_ref[sq, h] = s_new
            y_d = _norm_rows(o) * gd_ref[:, col(0, h)] * _silu(gat_ref[sq, :, col(2, h)])
            y_ref[sq, :, col(1, h)] = y_d.astype(BF16)


def rec_mix(qkv3, gat3, gt3, g_c, g_d, c0, n0, m0, s0, nseq, length):
    b, t, _ = qkv3.shape
    tok = lambda w: pl.BlockSpec((nseq, length, w), lambda bi, c: (bi, c, 0))
    st = lambda *shape: pl.BlockSpec((nseq, N_HEADS) + shape, lambda bi, c: (bi, 0, 0, 0))
    vec = pl.BlockSpec((1, BRANCH), lambda bi, c: (0, 0))
    hd = HEAD_DIM
    state_specs = [st(hd, hd), st(1, hd), st(1, GATE_LANES), st(hd, hd)]
    return pl.pallas_call(
        functools.partial(_rec_mix_body, nseq=nseq, length=length),
        grid=(b // nseq, t // length),
        in_specs=[tok(N_OPERAND_BLOCKS * BRANCH), tok(N_GATING_BLOCKS * BRANCH), tok(GATE_LANES), vec, vec] + state_specs,
        out_specs=[tok(2 * BRANCH)] + state_specs,
        out_shape=[jax.ShapeDtypeStruct((b, t, 2 * BRANCH), BF16),
                   jax.ShapeDtypeStruct((b, N_HEADS, hd, hd), F32),
                   jax.ShapeDtypeStruct((b, N_HEADS, 1, hd), F32),
                   jax.ShapeDtypeStruct((b, N_HEADS, 1, GATE_LANES), F32),
                   jax.ShapeDtypeStruct((b, N_HEADS, hd, hd), F32)],
        compiler_params=_ARB2,
        name="rec_mix",
    )(qkv3, gat3, gt3, g_c.reshape(1, -1), g_d.reshape(1, -1), c0, n0, m0, s0)


def _conv_layer(x_tm, sa, sb, g_pre, g_post, w_in, a_w, b_w, b_bias, ln_g, ln_b, w_out, tt, bb, tch):
    t, b, d = x_tm.shape
    x2 = x_tm.reshape(t * b, d)
    proj = norm_matmul(x2, g_pre, w_in, tm=min(IN_PROJ_ROWS, t * b), tn=1024)
    y, na, nb = conv_mix(proj.reshape(t, b, -1), sa, sb, a_w, b_w, b_bias, ln_g, ln_b, tt, bb, tch)
    x_out = out_proj(y.reshape(t * b, -1), w_out, g_post, x2, tm=512)
    return x_out.reshape(t, b, d), na, nb


def _rec_layer(x_bm, pos, c0, n0, m0, s0, g_pre, g_post, w_in, w_gate, gate_bias, g_c, g_d, w_out, nseq):
    b, t, d = x_bm.shape
    length = math.gcd(t, CHUNK)
    tm = min(REC_IN_PROJ_ROWS, b * t)
    x2 = x_bm.reshape(b * t, d)
    half = HEAD_DIM // 2
    inv = ROPE_BASE ** (-jnp.arange(half, dtype=F32) / half)
    ang = pos[:, None] * inv[None, :]
    reps = max(1, tm // t)
    cos_rows, sin_rows = jnp.tile(jnp.cos(ang), (reps, 1)), jnp.tile(jnp.sin(ang), (reps, 1))
    qkv, gat, gt = rec_in_proj(x2, g_pre, w_in, w_gate, gate_bias, cos_rows, sin_rows, tm)
    m0b = jnp.broadcast_to(m0[:, :, None, None], (b, N_HEADS, 1, GATE_LANES))
    y, c_new, n_new, m_new, s_new = rec_mix(
        qkv.reshape(b, t, -1), gat.reshape(b, t, -1), gt.reshape(b, t, -1), g_c, g_d,
        c0, n0[:, :, None, :], m0b, s0, nseq, length)
    x_out = out_proj(y.reshape(b * t, -1), w_out, g_post, x2, tm=512)
    return x_out.reshape(b, t, d), c_new, n_new[:, :, 0, :], m_new[:, :, 0, 0], s_new


def kernel(x_prompt, x_sample, state_a_conv, state_b_conv, state_c_C, state_c_n, state_c_m, state_d_S, norm_pre, norm_post, w_in_conv, a_conv_w, b_conv_w, b_conv_b, b_ln_g, b_ln_b, w_out_conv, w_in_rec, c_i_b, c_f_b, c_hn_g, d_hn_g, w_out_rec):
    bp, tp, _ = x_prompt.shape
    bs, ts, _ = x_sample.shape
    depth = norm_pre.shape[0]
    pos_p = jnp.arange(tp, dtype=F32)
    pos_s = PAST_LEN + jnp.arange(ts, dtype=F32)
    tm_axes = (1, 0, 2)

    outs = [[] for _ in range(12)]
    xp, xs = x_prompt, x_sample
    for layer in range(depth):
        p = layer // 2
        if layer % 2 == 0:
            wts = (norm_pre[layer], norm_post[layer], w_in_conv[p].astype(BF16), a_conv_w[p], b_conv_w[p],
                   b_conv_b[p], b_ln_g[p], b_ln_b[p], w_out_conv[p].astype(BF16))
            zero_a = jnp.zeros((A_CONV - 1, bp, BRANCH), F32)
            zero_b = jnp.zeros((B_CONV - 1, bp, BRANCH), F32)
            op, na_p, nb_p = _conv_layer(xp.transpose(tm_axes), zero_a, zero_b, *wts, tt=32, bb=8, tch=16)
            os_, na_s, nb_s = _conv_layer(xs.transpose(tm_axes), state_a_conv[p].transpose(tm_axes),
                                          state_b_conv[p].transpose(tm_axes), *wts, tt=ts, bb=16, tch=8)
            xp, xs = op.transpose(tm_axes), os_.transpose(tm_axes)
            new = (na_p.transpose(tm_axes), na_s.transpose(tm_axes), nb_p.transpose(tm_axes), nb_s.transpose(tm_axes))
            for lst, val in zip(outs[0:4], new):
                lst.append(val)
        else:
            w_full = w_in_rec[p]
            w_main = jnp.concatenate([w_full[:, i * BRANCH:(i + 1) * BRANCH] for i in REC_COL_ORDER], axis=1)
            n_main = len(REC_COL_ORDER) * BRANCH
            w_gate = jnp.pad(w_full[:, n_main:], ((0, 0), (0, GATE_LANES - 2 * N_HEADS)))
            gate_bias = jnp.pad(jnp.concatenate([c_i_b[p], c_f_b[p]]), (0, GATE_LANES - 2 * N_HEADS)).reshape(1, GATE_LANES)
            wts = (norm_pre[layer], norm_post[layer], w_main.astype(BF16), w_gate, gate_bias,
                   c_hn_g[p], d_hn_g[p], w_out_rec[p].astype(BF16))
            zc = jnp.zeros((bp, N_HEADS, HEAD_DIM, HEAD_DIM), F32)
            zn = jnp.zeros((bp, N_HEADS, HEAD_DIM), F32)
            zm = jnp.zeros((bp, N_HEADS), F32)
            xp, c1, n1, m1, s1 = _rec_layer(xp, pos_p, zc, zn, zm, zc, *wts, nseq=1)
            xs, c2, n2, m2, s2 = _rec_layer(xs, pos_s, state_c_C[p], state_c_n[p], state_c_m[p], state_d_S[p],
                                            *wts, nseq=4)
            for lst, val in zip(outs[4:12], (c1, c2, n1, n2, m1, m2, s1, s2)):
                lst.append(val)

    return (xp, xs) + tuple(jnp.stack(lst) for lst in outs)
```
